```python
import math
import jax, jax.numpy as jnp
from jax import lax
import numpy as np

D_MODEL = 2048
BATCH = 4
SEQ = 4096
DEPTH = 2

GRID_W = 64
CTX_LEN = 256

HEAD_DIM = 128
ATTN_HEADS = 8
ATTN_KV_HEADS = 2
ATTN_GROUP = ATTN_HEADS // ATTN_KV_HEADS
WINDOW = 128
ATTN_BLOCK = 128
ROPE_BASE = 10000.0
MASK_VALUE = -1e30
GM_HEADS = 4
GM_CHUNK = 128
DN_HEADS = 4
DN_CHUNK = 64
DN_CONV = 3
ATTN_W = ATTN_HEADS * HEAD_DIM
KV_W = ATTN_KV_HEADS * HEAD_DIM
GM_W = GM_HEADS * HEAD_DIM
DN_W = DN_HEADS * HEAD_DIM
MIX_W = ATTN_W + GM_W + DN_W
IN_W = ATTN_W + 2 * KV_W + 2 * GM_W + 4 * DN_W + 4 * DN_HEADS
N_EXPERTS = 16
EC_CAPACITY = 2
EXPERT_FF = D_MODEL // 2
NORM_EPS = 1e-6

kernel_name = 'hybrid_diffusion_gqa_gmlp_gdn_ecmoe'


def _rmsnorm(x, g):
    xf = x.astype(jnp.float32)
    y = xf * lax.rsqrt(jnp.mean(xf * xf, axis=-1, keepdims=True) + NORM_EPS)
    return (y * g.astype(jnp.float32)).astype(x.dtype)


def _layernorm(x, g):
    xf = x.astype(jnp.float32)
    xf = xf - jnp.mean(xf, axis=-1, keepdims=True)
    y = xf * lax.rsqrt(jnp.mean(xf * xf, axis=-1, keepdims=True) + NORM_EPS)
    return (y * g.astype(jnp.float32)).astype(x.dtype)


def _l2norm(x):
    return x * lax.rsqrt(jnp.sum(x * x, axis=-1, keepdims=True) + NORM_EPS)


def _rope_tables(n, dtype):
    rows = n // GRID_W
    r = jnp.repeat(jnp.arange(rows, dtype=jnp.float32), GRID_W)
    col = jnp.tile(jnp.arange(GRID_W, dtype=jnp.float32), rows)
    half = HEAD_DIM // 2
    inv = ROPE_BASE ** (-jnp.arange(0, half, 2, dtype=jnp.float32) / half)
    ang = jnp.concatenate([r[:, None] * inv, col[:, None] * inv], axis=-1)
    return jnp.cos(ang).astype(dtype), jnp.sin(ang).astype(dtype)


def _apply_axial_rope(t, cos, sin):
    B, n, H, _ = t.shape
    q = HEAD_DIM // 4
    t = t.reshape(B, n, H, 2, 2, q)
    c = cos.reshape(n, 2, q)[None, :, None]
    s = sin.reshape(n, 2, q)[None, :, None]
    t1, t2 = t[..., 0, :], t[..., 1, :]
    return jnp.stack([t1 * c - t2 * s, t2 * c + t1 * s], axis=-2).reshape(B, n, H, HEAD_DIM)


def _split_proj(p):
    sizes = (ATTN_W, KV_W, KV_W, GM_W, GM_W, 3 * DN_W, DN_W, 4 * DN_HEADS)
    idx = [int(i) for i in np.cumsum(sizes)[:-1]]
    return jnp.split(p, idx, axis=-1)


def _window_gqa(q, k, v, kc, vc, sink):
    B, n, _, dh = q.shape
    m = kc.shape[1]
    nb = n // ATTN_BLOCK
    qb = (q * dh ** -0.5).reshape(B, nb, ATTN_BLOCK, ATTN_KV_HEADS, ATTN_GROUP, dh)

    def band(t):
        tb = t.reshape(B, nb, ATTN_BLOCK, ATTN_KV_HEADS, dh)
        z = jnp.zeros_like(tb[:, :1])
        return jnp.concatenate([jnp.concatenate([z, tb[:, :-1]], axis=1), tb,
                                jnp.concatenate([tb[:, 1:], z], axis=1)], axis=2)

    kw, vw = band(k), band(v)
    s_win = jnp.einsum('bnqkgd,bnskd->bnkgqs', qb, kw, preferred_element_type=jnp.float32)
    qi = jnp.arange(ATTN_BLOCK)[:, None]
    sj = jnp.arange(3 * ATTN_BLOCK)[None, :] - ATTN_BLOCK
    kpos = jnp.arange(nb)[:, None, None] * ATTN_BLOCK + sj
    allowed = (jnp.abs(sj - qi) <= WINDOW)[None] & (kpos >= 0) & (kpos < n)
    s_win = jnp.where(allowed[None, :, None, None], s_win, MASK_VALUE)
    s_ctx = jnp.einsum('bnqkgd,bckd->bnkgqc', qb, kc, preferred_element_type=jnp.float32)
    sink_l = jnp.broadcast_to(sink.astype(jnp.float32).reshape(ATTN_KV_HEADS, ATTN_GROUP)[None, None, :, :, None, None],
                              s_win.shape[:-1] + (1,))
    p = jax.nn.softmax(jnp.concatenate([s_win, s_ctx, sink_l], axis=-1), axis=-1)
    pw = p[..., :3 * ATTN_BLOCK].astype(v.dtype)
    pc = p[..., 3 * ATTN_BLOCK:3 * ATTN_BLOCK + m].astype(v.dtype)
    o = jnp.einsum('bnkgqs,bnskd->bnqkgd', pw, vw) + jnp.einsum('bnkgqc,bckd->bnqkgd', pc, vc)
    return o.reshape(B, n, ATTN_W)


def _context_gqa(qc, kc, vc, sink):
    B, m, _, dh = qc.shape
    qs = (qc * dh ** -0.5).reshape(B, m, ATTN_KV_HEADS, ATTN_GROUP, dh)
    s = jnp.einsum('bqkgd,bckd->bkgqc', qs, kc, preferred_element_type=jnp.float32)
    sink_c = jnp.broadcast_to(sink.astype(jnp.float32).reshape(ATTN_KV_HEADS, ATTN_GROUP)[None, :, :, None, None],
                              s.shape[:-1] + (1,))
    p = jax.nn.softmax(jnp.concatenate([s, sink_c], axis=-1), axis=-1)[..., :m].astype(vc.dtype)
    return jnp.einsum('bkgqc,bckd->bqkgd', p, vc).reshape(B, m, ATTN_W)


def _chunk_gmlp(u, v, ln_g, ws, bs):
    B, n, _ = u.shape
    nc = n // GM_CHUNK
    u = jax.nn.gelu(u).reshape(B, nc, GM_CHUNK, GM_HEADS, HEAD_DIM)
    v = _layernorm(jax.nn.gelu(v).reshape(B, nc, GM_CHUNK, GM_HEADS, HEAD_DIM), ln_g)
    mixed = jnp.einsum('hpq,bnqhc->bnphc', ws, v) + bs.T[:, :, None]
    return (u * mixed).reshape(B, n, GM_W)


def _short_conv(x, w):
    C = x.shape[-1]
    y = lax.conv_general_dilated(x, w[:, None, :].astype(x.dtype), window_strides=(1,),
                                 padding=[(DN_CONV // 2, DN_CONV // 2)],
                                 dimension_numbers=('NWC', 'WIO', 'NWC'), feature_group_count=C)
    return jax.nn.silu(y)


def _delta_inputs(p_qkv, p_gate, conv_w, a_log, dt_bias):
    B, n, _ = p_qkv.shape
    qkv = _short_conv(p_qkv, conv_w).astype(jnp.float32)
    q, k, v = [t.reshape(B, n, DN_HEADS, HEAD_DIM) for t in jnp.split(qkv, 3, axis=-1)]
    q = _l2norm(q) * HEAD_DIM ** -0.5
    k = _l2norm(k)
    gate = p_gate.astype(jnp.float32).reshape(B, n, 2, 2, DN_HEADS)
    a, b = gate[..., 0, :], gate[..., 1, :]
    g = -jnp.exp(a_log.astype(jnp.float32)) * jax.nn.softplus(a + dt_bias.astype(jnp.float32))
    beta = jax.nn.sigmoid(b)
    return q, k, v, g, beta


def _gated_delta_chunked(q, k, v, g, beta, state):
    B, n, H, dk = q.shape
    dv = v.shape[-1]
    L = DN_CHUNK
    nc = n // L

    def chunks(t):
        t = t.reshape((B, nc, L, H) + t.shape[3:])
        return jnp.moveaxis(t, (1, 3), (0, 2))

    qc, kc, vc = chunks(q), chunks(k), chunks(v)
    gc = jnp.cumsum(chunks(g), axis=-1)
    bc = chunks(beta)
    kb = kc * bc[..., None]
    i = jnp.arange(L)
    incl = i[:, None] >= i[None, :]
    strict = i[:, None] > i[None, :]
    decay = jnp.exp(jnp.where(incl, gc[..., :, None] - gc[..., None, :], -jnp.inf))
    a = jnp.where(strict, jnp.einsum('...id,...jd->...ij', kb, kc) * decay, 0.0)
    rhs = jnp.concatenate([vc * bc[..., None], kb * jnp.exp(gc)[..., None]], axis=-1)
    sol = lax.linalg.triangular_solve(jnp.eye(L, dtype=a.dtype) + a, rhs, left_side=True, lower=True,
                                      unit_diagonal=True)
    u, w = sol[..., :dv], sol[..., dv:]

    def step(S, inp):
        qi, ki, ui, wi, gi, di = inp
        v_new = ui - jnp.einsum('bhlk,bhkv->bhlv', wi, S)
        intra = jnp.einsum('bhik,bhjk->bhij', qi, ki) * di
        o = jnp.einsum('bhlk,bhkv->bhlv', qi * jnp.exp(gi)[..., None], S) + jnp.einsum('bhij,bhjv->bhiv', intra, v_new)
        g_last = gi[..., -1:]
        S = S * jnp.exp(g_last)[..., None] + jnp.einsum('bhlk,bhlv->bhkv', ki * jnp.exp(g_last - gi)[..., None], v_new)
        return S, o

    state, o = lax.scan(step, state, (qc, kc, u, w, gc, decay))
    return jnp.moveaxis(o, (0, 2), (1, 3)).reshape(B, n, H, dv), state


def _run_direction(t, d, state, reverse):
    q, k, v, g, beta = t
    g_d, b_d = g[:, :, d], beta[:, :, d]
    if reverse:
        q, k, v, g_d, b_d = [jnp.flip(a, axis=1) for a in (q, k, v, g_d, b_d)]
    o, state = _gated_delta_chunked(q, k, v, g_d, b_d, state)
    if reverse:
        o = jnp.flip(o, axis=1)
    return o, state


def _bidir_delta(lat, ctx):
    B = lat[0].shape[0]
    s0 = jnp.zeros((B, DN_HEADS, HEAD_DIM, HEAD_DIM), jnp.float32)
    oc_f, s_f = _run_direction(ctx, 0, s0, False)
    oc_b, s_b = _run_direction(ctx, 1, s0, True)
    o_f, _ = _run_direction(lat, 0, s_f, False)
    o_b, _ = _run_direction(lat, 1, s_b, True)
    return o_f + o_b, oc_f + oc_b


def _gated_out(o, z, g):
    B, n = z.shape[:2]
    zz = z.astype(jnp.float32).reshape(o.shape)
    return (_rmsnorm(o, g) * jax.nn.silu(zz)).reshape(B, n, DN_W).astype(z.dtype)


def _token_mixers(p, pc, cos, sin, attn_sink, gm_ln, gm_ws, gm_bs, dn_conv, dn_a_log, dn_dt_bias, dn_norm, need_ctx):
    B, n, _ = p.shape
    m = pc.shape[1]
    aq, ak, av, gu, gv, dqkv, dz, dgate = _split_proj(p)
    aqc, akc, avc, guc, gvc, dqkvc, dzc, dgatec = _split_proj(pc)
    q = _apply_axial_rope(aq.reshape(B, n, ATTN_HEADS, HEAD_DIM), cos, sin)
    k = _apply_axial_rope(ak.reshape(B, n, ATTN_KV_HEADS, HEAD_DIM), cos, sin)
    v = av.reshape(B, n, ATTN_KV_HEADS, HEAD_DIM)
    kc = akc.reshape(B, m, ATTN_KV_HEADS, HEAD_DIM)
    vc = avc.reshape(B, m, ATTN_KV_HEADS, HEAD_DIM)
    y_attn = _window_gqa(q, k, v, kc, vc, attn_sink)
    y_gm = _chunk_gmlp(gu, gv, gm_ln, gm_ws, gm_bs)
    lat = _delta_inputs(dqkv, dgate, dn_conv, dn_a_log, dn_dt_bias)
    ctx = _delta_inputs(dqkvc, dgatec, dn_conv, dn_a_log, dn_dt_bias)
    o_dn, oc_dn = _bidir_delta(lat, ctx)
    y = jnp.concatenate([y_attn, y_gm, _gated_out(o_dn, dz, dn_norm)], axis=-1)
    if not need_ctx:
        return y, None
    yc = jnp.concatenate([_context_gqa(aqc.reshape(B, m, ATTN_HEADS, HEAD_DIM), kc, vc, attn_sink),
                          _chunk_gmlp(guc, gvc, gm_ln, gm_ws, gm_bs),
                          _gated_out(oc_dn, dzc, dn_norm)], axis=-1)
    return y, yc


def _expert_choice_ffn(h, w_router, w_e1, w_e3, w_e2):
    B, n, D = h.shape
    cap = EC_CAPACITY * n // N_EXPERTS
    aff = jax.nn.softmax(jnp.einsum('bnd,de->ben', h, w_router).astype(jnp.float32), axis=1)
    gate, idx = lax.top_k(aff, cap)
    xs = jax.vmap(lambda hb, ib: hb[ib])(h, idx)
    a = jnp.einsum('becd,edf->becf', xs, w_e1)
    b = jnp.einsum('becd,edf->becf', xs, w_e3)
    y = jnp.einsum('becf,efd->becd', jax.nn.silu(a) * b, w_e2) * gate[..., None].astype(h.dtype)
    scatter = lambda ib, yb: jnp.zeros((n, D), h.dtype).at[ib.reshape(-1)].add(yb.reshape(-1, D))
    return jax.vmap(scatter)(idx, y)


def _layer(x, xc, c, c_ctx, cos, sin, g_pre1, g_post1, g_pre2, g_post2, w_mod, b_mod, w_in, attn_sink,
           gm_ln, gm_ws, gm_bs, dn_conv, dn_a_log, dn_dt_bias, dn_norm, w_out, w_router, w_e1, w_e3, w_e2,
           update_ctx):
    mod = jax.nn.silu(c) @ w_mod + b_mod
    mod_c = jax.nn.silu(c_ctx) @ w_mod + b_mod
    sh1, sc1, ga1, sh2, sc2, ga2 = [t[:, None] for t in jnp.split(mod, 6, axis=-1)]
    sh1c, sc1c, ga1c, sh2c, sc2c, ga2c = jnp.split(mod_c, 6, axis=-1)
    h = _rmsnorm(x, g_pre1) * (1 + sc1) + sh1
    hc = _rmsnorm(xc, g_pre1) * (1 + sc1c) + sh1c
    y, yc = _token_mixers(h @ w_in, hc @ w_in, cos, sin, attn_sink, gm_ln, gm_ws, gm_bs, dn_conv,
                          dn_a_log, dn_dt_bias, dn_norm, update_ctx)
    x = x + ga1 * _rmsnorm(y @ w_out, g_post1)
    h2 = _rmsnorm(x, g_pre2) * (1 + sc2) + sh2
    x = x + ga2 * _rmsnorm(_expert_choice_ffn(h2, w_router, w_e1, w_e3, w_e2), g_post2)
    if update_ctx:
        xc = xc + ga1c * _rmsnorm(yc @ w_out, g_post1)
        h2c = _rmsnorm(xc, g_pre2) * (1 + sc2c) + sh2c
        xc = xc + ga2c * _rmsnorm(_expert_choice_ffn(h2c, w_router, w_e1, w_e3, w_e2), g_post2)
    return x, xc


def setup_inputs(seed: int = 0) -> dict:
    key = jax.random.key(seed)
    ks = jax.random.split(key, 26)
    D = D_MODEL

    def nrm(k, shape, scale):
        return jax.random.normal(k, shape, jnp.float32) * scale

    dt = jnp.exp(jax.random.uniform(ks[18], (DEPTH, 2, DN_HEADS), jnp.float32,
                                    minval=math.log(1e-3), maxval=math.log(0.1)))
    return {
        'x': nrm(ks[0], (BATCH, SEQ, D), 1.0),
        'c': nrm(ks[1], (BATCH, D), 1.0),
        'ctx': nrm(ks[2], (BATCH, CTX_LEN, D), 1.0),
        'c_ctx': nrm(ks[3], (D,), 1.0),
        'g_pre1': 1.0 + nrm(ks[4], (DEPTH, D), 0.05),
        'g_post1': 1.0 + nrm(ks[5], (DEPTH, D), 0.05),
        'g_pre2': 1.0 + nrm(ks[6], (DEPTH, D), 0.05),
        'g_post2': 1.0 + nrm(ks[7], (DEPTH, D), 0.05),
        'w_mod': nrm(ks[8], (DEPTH, D, 6 * D), 0.5 * D ** -0.5),
        'b_mod': nrm(ks[9], (DEPTH, 6 * D), 0.02),
        'w_in': nrm(ks[10], (DEPTH, D, IN_W), D ** -0.5),
        'attn_sink': nrm(ks[11], (DEPTH, ATTN_HEADS), 0.5),
        'gm_ln': 1.0 + nrm(ks[12], (DEPTH, GM_HEADS, HEAD_DIM), 0.05),
        'gm_ws': nrm(ks[13], (DEPTH, GM_HEADS, GM_CHUNK, GM_CHUNK), GM_CHUNK ** -0.5),
        'gm_bs': 1.0 + nrm(ks[14], (DEPTH, GM_HEADS, GM_CHUNK), 0.05),
        'dn_conv': nrm(ks[15], (DEPTH, DN_CONV, 3 * DN_W), DN_CONV ** -0.5),
        'dn_a_log': jnp.log(jax.random.uniform(ks[16], (DEPTH, 2, DN_HEADS), jnp.float32, minval=1.0, maxval=16.0)),
        'dn_dt_bias': dt + jnp.log(-jnp.expm1(-dt)),
        'dn_norm': 1.0 + nrm(ks[17], (DEPTH, HEAD_DIM), 0.05),
        'w_out': nrm(ks[19], (DEPTH, MIX_W, D), MIX_W ** -0.5),
        'w_router': nrm(ks[20], (DEPTH, D, N_EXPERTS), D ** -0.5),
        'w_e1': nrm(ks[21], (DEPTH, N_EXPERTS, D, EXPERT_FF), D ** -0.5),
        'w_e3': nrm(ks[22], (DEPTH, N_EXPERTS, D, EXPERT_FF), D ** -0.5),
        'w_e2': nrm(ks[23], (DEPTH, N_EXPERTS, EXPERT_FF, D), EXPERT_FF ** -0.5),
    }


def reference(x, c, ctx, c_ctx, g_pre1, g_post1, g_pre2, g_post2, w_mod, b_mod, w_in, attn_sink,
              gm_ln, gm_ws, gm_bs, dn_conv, dn_a_log, dn_dt_bias, dn_norm, w_out, w_router, w_e1, w_e3, w_e2):
    n = x.shape[1]
    cos, sin = _rope_tables(n, x.dtype)
    xc = ctx
    for l in range(DEPTH):
        x, xc = _layer(x, xc, c, c_ctx, cos, sin, g_pre1[l], g_post1[l], g_pre2[l], g_post2[l], w_mod[l], b_mod[l],
                       w_in[l], attn_sink[l], gm_ln[l], gm_ws[l], gm_bs[l], dn_conv[l], dn_a_log[l], dn_dt_bias[l],
                       dn_norm[l], w_out[l], w_router[l], w_e1[l], w_e3[l], w_e2[l], l < DEPTH - 1)
    return x
```

```python
import functools

import jax
import jax.numpy as jnp
from jax import lax
from jax.experimental import pallas as pl
from jax.experimental.pallas import tpu as pltpu

F32 = jnp.float32
BF16 = jnp.bfloat16
I32 = jnp.int32

HEAD_DIM = 128
GRID_W = 64
ATTN_HEADS = 8
ATTN_KV_HEADS = 2
ATTN_GROUP = ATTN_HEADS // ATTN_KV_HEADS
WINDOW = 128
ATTN_BLOCK = 128
ROPE_BASE = 10000.0
MASK_VALUE = -1e30
GM_HEADS = 4
GM_CHUNK = 128
DN_HEADS = 4
DN_CHUNK = 64
N_EXPERTS = 16
EC_CAPACITY = 2
NORM_EPS = 1e-6

ATTN_W = ATTN_HEADS * HEAD_DIM
KV_W = ATTN_KV_HEADS * HEAD_DIM
QKV_W = ATTN_W + 2 * KV_W
GM_W = GM_HEADS * HEAD_DIM
DN_W = DN_HEADS * HEAD_DIM
MIX_W = ATTN_W + GM_W + DN_W
REST_GU, REST_GV, REST_DQ, REST_DK, REST_DV, REST_DZ, REST_GATE = 0, 512, 1024, 1536, 2048, 2560, 3072
REST_W = 3200
IN_PAD_W = QKV_W + REST_W

V7X_VMEM_LIMIT = 56 * 1024 * 1024
STEP = 128


def _cparams(sem, vmem=None):
    return pltpu.CompilerParams(dimension_semantics=sem, vmem_limit_bytes=vmem)


def _silu(x):
    return x * jax.nn.sigmoid(x)


def _dot(a, b):
    return jnp.dot(a, b, preferred_element_type=F32)


def _dot_nt(a, b):
    return lax.dot_general(a, b, (((1,), (1,)), ((), ())), preferred_element_type=F32)


def _split2(a):
    hi = a.astype(BF16)
    lo = (a - hi.astype(F32)).astype(BF16)
    return hi, lo


def _split3(a):
    hi = a.astype(BF16)
    r = a - hi.astype(F32)
    mid = r.astype(BF16)
    lo = (r - mid.astype(F32)).astype(BF16)
    return hi, mid, lo


def _dot_x3(a, b, nt=False):
    d = _dot_nt if nt else _dot
    ah, al = _split2(a)
    bh, bl = _split2(b)
    return d(ah, bh) + d(ah, bl) + d(al, bh)


def _mod_body(c_ref, w_ref, b_ref, o_ref):
    s = _silu(c_ref[...]).astype(BF16)
    o_ref[...] = _dot(s, w_ref[...].astype(BF16)) + b_ref[...]


def _modulation(c8, w_mod, b_mod):
    d, n6 = w_mod.shape
    tn = 1024
    return pl.pallas_call(
        _mod_body,
        grid=(n6 // tn,),
        in_specs=[pl.BlockSpec((8, d), lambda j: (0, 0)),
                  pl.BlockSpec((d, tn), lambda j: (0, j)),
                  pl.BlockSpec((1, tn), lambda j: (0, j))],
        out_specs=pl.BlockSpec((8, tn), lambda j: (0, j)),
        out_shape=jax.ShapeDtypeStruct((8, n6), F32),
        compiler_params=_cparams(("arbitrary",), 40 * 1024 * 1024),
        name="modulation",
    )(c8, w_mod, b_mod.reshape(1, n6))


def _inproj_body(*refs, rope, tm):
    if rope:
        x_ref, mod_ref, g_ref, w_ref, cos_ref, sin_ref, qkv_ref, rest_ref = refs
    else:
        x_ref, mod_ref, g_ref, w_ref, qkv_ref, rest_ref = refs
    x = x_ref[0]
    ms = jnp.mean(x * x, axis=-1, keepdims=True)
    y = x * lax.rsqrt(ms + NORM_EPS) * g_ref[...]
    h = y * (1.0 + mod_ref[0, 1:2, :]) + mod_ref[0, 0:1, :]
    hb = h.astype(BF16)
    if rope:
        cosf = cos_ref[...]
        sinf = sin_ref[...]
        lane = lax.broadcasted_iota(I32, (tm, HEAD_DIM), 1)
        first = (lane & 32) == 0

    def rot(t):
        if not rope:
            return t
        partner = jnp.where(first, pltpu.roll(t, 96, 1), pltpu.roll(t, 32, 1))
        return t * cosf + partner * sinf

    for c0 in (0, 512):
        t = _dot(hb, w_ref[:, c0:c0 + 512])
        for j in range(4):
            th = rot(t[:, j * 128:(j + 1) * 128]) * (HEAD_DIM ** -0.5)
            qkv_ref[0, :, c0 + j * 128:c0 + (j + 1) * 128] = th.astype(BF16)
    t = _dot(hb, w_ref[:, ATTN_W:ATTN_W + 512])
    for j in range(2):
        qkv_ref[0, :, ATTN_W + j * 128:ATTN_W + (j + 1) * 128] = rot(t[:, j * 128:(j + 1) * 128]).astype(BF16)
    qkv_ref[0, :, ATTN_W + KV_W:QKV_W] = t[:, 256:512].astype(BF16)
    for c0 in range(0, REST_W, 512):
        cw = min(512, REST_W - c0)
        rest_ref[0, :, c0:c0 + cw] = _dot(hb, w_ref[:, QKV_W + c0:QKV_W + c0 + cw])


def _inproj(x, mod, g, w_bf, rope_cs):
    b, t, d = x.shape
    tm = 256
    rope = rope_cs is not None
    mod_map = (lambda bi, i: (bi, 0, 0)) if mod.shape[0] == b else (lambda bi, i: (0, 0, 0))
    in_specs = [pl.BlockSpec((1, tm, d), lambda bi, i: (bi, i, 0)),
                pl.BlockSpec((1, 6, d), mod_map),
                pl.BlockSpec((1, d), lambda bi, i: (0, 0)),
                pl.BlockSpec((d, IN_PAD_W), lambda bi, i: (0, 0), pipeline_mode=pl.Buffered(1))]
    args = [x, mod, g.reshape(1, d), w_bf]
    if rope:
        in_specs += [pl.BlockSpec((tm, HEAD_DIM), lambda bi, i: (i, 0)),
                     pl.BlockSpec((tm, HEAD_DIM), lambda bi, i: (i, 0))]
        args += list(rope_cs)
    return pl.pallas_call(
        functools.partial(_inproj_body, rope=rope, tm=tm),
        grid=(b, t // tm),
        in_specs=in_specs,
        out_specs=[pl.BlockSpec((1, tm, QKV_W), lambda bi, i: (bi, i, 0)),
                   pl.BlockSpec((1, tm, REST_W), lambda bi, i: (bi, i, 0))],
        out_shape=[jax.ShapeDtypeStruct((b, t, QKV_W), BF16),
                   jax.ShapeDtypeStruct((b, t, REST_W), F32)],
        compiler_params=_cparams(("parallel", "arbitrary"), V7X_VMEM_LIMIT),
        name="inproj_rope" if rope else "inproj",
    )(*args)


def _softmax_pv(sink_ref, kh, parts):
    sink_col = jnp.concatenate(
        [jnp.full((ATTN_BLOCK, 1), sink_ref[kh * ATTN_GROUP + g], F32) for g in range(ATTN_GROUP)], axis=0)
    mx = sink_col
    for s, _ in parts:
        mx = jnp.maximum(mx, jnp.max(s, axis=-1, keepdims=True))
    den = jnp.exp(sink_col - mx)
    acc = None
    for s, v in parts:
        e = jnp.exp(s - mx)
        den = den + jnp.sum(e, axis=-1, keepdims=True)
        pv = _dot(e.astype(BF16), v)
        acc = pv if acc is None else acc + pv
    return acc / den


def _attn_body(sink_ref, q_ref, km_ref, vm_ref, kp_ref, vp_ref, kn_ref, vn_ref, kc_ref, vc_ref, o_ref, *, tq, n):
    i = pl.program_id(1)
    qb = tq // ATTN_BLOCK
    qi = lax.broadcasted_iota(I32, (ATTN_BLOCK, 3 * ATTN_BLOCK), 0)
    sj = lax.broadcasted_iota(I32, (ATTN_BLOCK, 3 * ATTN_BLOCK), 1) - ATTN_BLOCK
    for j in range(qb):
        kpos = (i * qb + j) * ATTN_BLOCK + sj
        ok = (jnp.abs(sj - qi) <= WINDOW) & (kpos >= 0) & (kpos < n)
        bias = jnp.where(ok, 0.0, MASK_VALUE).astype(F32)
        bias4 = jnp.concatenate([bias] * ATTN_GROUP, axis=0)
        rows = slice(j * ATTN_BLOCK, (j + 1) * ATTN_BLOCK)
        for kh in range(ATTN_KV_HEADS):
            hs = slice(kh * HEAD_DIM, (kh + 1) * HEAD_DIM)

            def band(main_ref, prev_ref, next_ref):
                pieces = []
                for blk in (j - 1, j, j + 1):
                    if blk < 0:
                        pieces.append(prev_ref[0, :, hs])
                    elif blk >= qb:
                        pieces.append(next_ref[0, :, hs])
                    else:
                        pieces.append(main_ref[0, blk * ATTN_BLOCK:(blk + 1) * ATTN_BLOCK, hs])
                return jnp.concatenate(pieces, axis=0)

            kb = band(km_ref, kp_ref, kn_ref)
            vb = band(vm_ref, vp_ref, vn_ref)
            q = jnp.concatenate(
                [q_ref[0, rows, (kh * ATTN_GROUP + g) * HEAD_DIM:(kh * ATTN_GROUP + g + 1) * HEAD_DIM]
                 for g in range(ATTN_GROUP)], axis=0)
            s_win = _dot_nt(q, kb) + bias4
            s_ctx = _dot_nt(q, kc_ref[0, :, hs])
            o = _softmax_pv(sink_ref, kh, [(s_win, vb), (s_ctx, vc_ref[0, :, hs])])
            for g in range(ATTN_GROUP):
                hh = kh * ATTN_GROUP + g
                o_ref[0, rows, hh * HEAD_DIM:(hh + 1) * HEAD_DIM] = (
                    o[g * ATTN_BLOCK:(g + 1) * ATTN_BLOCK].astype(BF16))


def _attention(qkv, qkvc, sink):
    b, n, _ = qkv.shape
    m = qkvc.shape[1]
    tq = 256
    qb = tq // ATTN_BLOCK
    nb = n // ATTN_BLOCK
    kcol, vcol = ATTN_W // KV_W, ATTN_W // KV_W + 1
    smem = pl.BlockSpec(memory_space=pltpu.SMEM)
    main = lambda col: pl.BlockSpec((1, tq, KV_W), lambda bi, i: (bi, i, col))
    prev = lambda col: pl.BlockSpec((1, ATTN_BLOCK, KV_W), lambda bi, i: (bi, jnp.maximum(i * qb - 1, 0), col))
    nxt = lambda col: pl.BlockSpec((1, ATTN_BLOCK, KV_W), lambda bi, i: (bi, jnp.minimum(i * qb + qb, nb - 1), col))
    ctx = lambda col: pl.BlockSpec((1, m, KV_W), lambda bi, i: (bi, 0, col))
    return pl.pallas_call(
        functools.partial(_attn_body, tq=tq, n=n),
        grid=(b, n // tq),
        in_specs=[smem, pl.BlockSpec((1, tq, ATTN_W), lambda bi, i: (bi, i, 0)),
                  main(kcol), main(vcol), prev(kcol), prev(vcol), nxt(kcol), nxt(vcol), ctx(kcol), ctx(vcol)],
        out_specs=pl.BlockSpec((1, tq, ATTN_W), lambda bi, i: (bi, i, 0)),
        out_shape=jax.ShapeDtypeStruct((b, n, ATTN_W), BF16),
        compiler_params=_cparams(("parallel", "arbitrary"), 40 * 1024 * 1024),
        name="window_attention",
    )(sink, qkv, qkv, qkv, qkv, qkv, qkv, qkv, qkvc, qkvc)


def _ctx_attn_body(sink_ref, q_ref, kc_ref, vc_ref, o_ref, *, m):
    for j in range(m // ATTN_BLOCK):
        rows = slice(j * ATTN_BLOCK, (j + 1) * ATTN_BLOCK)
        for kh in range(ATTN_KV_HEADS):
            hs = slice(kh * HEAD_DIM, (kh + 1) * HEAD_DIM)
            q = jnp.concatenate(
                [q_ref[0, rows, (kh * ATTN_GROUP + g) * HEAD_DIM:(kh * ATTN_GROUP + g + 1) * HEAD_DIM]
                 for g in range(ATTN_GROUP)], axis=0)
            s_ctx = _dot_nt(q, kc_ref[0, :, hs])
            o = _softmax_pv(sink_ref, kh, [(s_ctx, vc_ref[0, :, hs])])
            for g in range(ATTN_GROUP):
                hh = kh * ATTN_GROUP + g
                o_ref[0, rows, hh * HEAD_DIM:(hh + 1) * HEAD_DIM] = (
                    o[g * ATTN_BLOCK:(g + 1) * ATTN_BLOCK].astype(BF16))


def _ctx_attention(qkvc, sink):
    b, m, _ = qkvc.shape
    kcol, vcol = ATTN_W // KV_W, ATTN_W // KV_W + 1
    return pl.pallas_call(
        functools.partial(_ctx_attn_body, m=m),
        grid=(b,),
        in_specs=[pl.BlockSpec(memory_space=pltpu.SMEM),
                  pl.BlockSpec((1, m, ATTN_W), lambda bi: (bi, 0, 0)),
                  pl.BlockSpec((1, m, KV_W), lambda bi: (bi, 0, kcol)),
                  pl.BlockSpec((1, m, KV_W), lambda bi: (bi, 0, vcol))],
        out_specs=pl.BlockSpec((1, m, ATTN_W), lambda bi: (bi, 0, 0)),
        out_shape=jax.ShapeDtypeStruct((b, m, ATTN_W), BF16),
        compiler_params=_cparams(("parallel",), 40 * 1024 * 1024),
        name="context_attention",
    )(sink, qkvc, qkvc, qkvc)


def _gmlp_body(u_ref, v_ref, ln_ref, ws_ref, bst_ref, o_ref, *, tg):
    for ci in range(tg // GM_CHUNK):
        rows = slice(ci * GM_CHUNK, (ci + 1) * GM_CHUNK)
        u = jax.nn.gelu(u_ref[0, rows, :])
        v = jax.nn.gelu(v_ref[0, rows, :])
        for h in range(GM_HEADS):
            hs = slice(h * HEAD_DIM, (h + 1) * HEAD_DIM)
            vh = v[:, hs]
            vh = vh - jnp.mean(vh, axis=-1, keepdims=True)
            vh = vh * lax.rsqrt(jnp.mean(vh * vh, axis=-1, keepdims=True) + NORM_EPS) * ln_ref[:, hs]
            mixed = _dot(ws_ref[h], vh.astype(BF16)) + bst_ref[:, h:h + 1]
            o_ref[0, rows, hs] = (u[:, hs] * mixed).astype(BF16)


def _gmlp(rest, gm_ln, gm_ws_bf, gm_bs):
    b, n, _ = rest.shape
    tg = min(512, n)
    return pl.pallas_call(
        functools.partial(_gmlp_body, tg=tg),
        grid=(b, n // tg),
        in_specs=[pl.BlockSpec((1, tg, GM_W), lambda bi, i: (bi, i, REST_GU // GM_W)),
                  pl.BlockSpec((1, tg, GM_W), lambda bi, i: (bi, i, REST_GV // GM_W)),
                  pl.BlockSpec((1, GM_W), lambda bi, i: (0, 0)),
                  pl.BlockSpec((GM_HEADS, GM_CHUNK, GM_CHUNK), lambda bi, i: (0, 0, 0)),
                  pl.BlockSpec((GM_CHUNK, GM_HEADS), lambda bi, i: (0, 0))],
        out_specs=pl.BlockSpec((1, tg, GM_W), lambda bi, i: (bi, i, 0)),
        out_shape=jax.ShapeDtypeStruct((b, n, GM_W), BF16),
        compiler_params=_cparams(("parallel", "arbitrary"), 40 * 1024 * 1024),
        name="gmlp",
    )(rest, rest, gm_ln.reshape(1, GM_W), gm_ws_bf, gm_bs.T)


def _dn_pre_body(alog_ref, dtb_ref, cw_ref, q_ref, k_ref, v_ref, qp_ref, kp_ref, vp_ref,
                 qn_ref, kn_ref, vn_ref, gate_ref, oq_ref, ok_ref, ov_ref, g_ref, gt_ref, *, td, nt):
    i = pl.program_id(1)
    row = lax.broadcasted_iota(I32, (td, DN_W), 0)
    has_prev = (i > 0).astype(F32)
    has_next = (i < nt - 1).astype(F32)

    def conv(x_ref, p_ref, n_ref, c0):
        x = x_ref[0]
        prev_row = p_ref[0, 7:8, :] * has_prev
        next_row = n_ref[0, 0:1, :] * has_next
        xm = jnp.where(row == 0, prev_row, pltpu.roll(x, 1, 0))
        xp = jnp.where(row == td - 1, next_row, pltpu.roll(x, td - 1, 0))
        y = (cw_ref[0:1, c0:c0 + DN_W] * xm + cw_ref[1:2, c0:c0 + DN_W] * x
             + cw_ref[2:3, c0:c0 + DN_W] * xp)
        return _silu(y)

    def l2n(y, h):
        yh = y[:, h * HEAD_DIM:(h + 1) * HEAD_DIM]
        return yh * lax.rsqrt(jnp.sum(yh * yh, axis=-1, keepdims=True) + NORM_EPS)

    yq = conv(q_ref, qp_ref, qn_ref, 0)
    yk = conv(k_ref, kp_ref, kn_ref, DN_W)
    for h in range(DN_HEADS):
        hs = slice(h * HEAD_DIM, (h + 1) * HEAD_DIM)
        oq_ref[0, :, hs] = l2n(yq, h) * (HEAD_DIM ** -0.5)
        ok_ref[0, :, hs] = l2n(yk, h)
    ov_ref[0] = conv(v_ref, vp_ref, vn_ref, 2 * DN_W)

    raw = gate_ref[0]
    z = raw + dtb_ref[...]
    softplus = jnp.maximum(z, 0.0) + jnp.log(1.0 + jnp.exp(-jnp.abs(z)))
    gval = -jnp.exp(alog_ref[...]) * softplus
    lane = lax.broadcasted_iota(I32, (td, 128), 1)
    gates = jnp.where(lane < 2 * DN_HEADS, gval, jax.nn.sigmoid(raw))
    g_ref[0] = gates
    gt_ref[0] = gates.T[0:16, :]


def _dn_pre(rest, alog_row, dtb_row, conv_w):
    b, n, _ = rest.shape
    td = 256
    nt = n // td
    cur = lambda col: pl.BlockSpec((1, td, DN_W), lambda bi, i: (bi, i, col))
    prev = lambda col: pl.BlockSpec((1, 8, DN_W), lambda bi, i: (bi, jnp.maximum(i * (td // 8) - 1, 0), col))
    nxt = lambda col: pl.BlockSpec((1, 8, DN_W), lambda bi, i: (bi, jnp.minimum((i + 1) * (td // 8), n // 8 - 1), col))
    cq, ck, cv = REST_DQ // DN_W, REST_DK // DN_W, REST_DV // DN_W
    row128 = pl.BlockSpec((1, 128), lambda bi, i: (0, 0))
    tok = lambda w: pl.BlockSpec((1, td, w), lambda bi, i: (bi, i, 0))
    return pl.pallas_call(
        functools.partial(_dn_pre_body, td=td, nt=nt),
        grid=(b, nt),
        in_specs=[row128, row128, pl.BlockSpec((3, 3 * DN_W), lambda bi, i: (0, 0)),
                  cur(cq), cur(ck), cur(cv), prev(cq), prev(ck), prev(cv), nxt(cq), nxt(ck), nxt(cv),
                  pl.BlockSpec((1, td, 128), lambda bi, i: (bi, i, REST_GATE // 128))],
        out_specs=[tok(DN_W), tok(DN_W), tok(DN_W), tok(128),
                   pl.BlockSpec((1, 16, td), lambda bi, i: (bi, 0, i))],
        out_shape=[jax.ShapeDtypeStruct((b, n, DN_W), F32)] * 3
        + [jax.ShapeDtypeStruct((b, n, 128), F32), jax.ShapeDtypeStruct((b, 16, n), F32)],
        compiler_params=_cparams(("parallel", "arbitrary"), 40 * 1024 * 1024),
        name="deltanet_pre",
    )(alog_row, dtb_row, conv_w, rest, rest, rest, rest, rest, rest, rest, rest, rest, rest)


def _unit_tri_inverse(a, eye):
    nmat = -a
    x = eye + nmat
    p = nmat
    for _ in range(5):
        pb = p.astype(BF16)
        p = _dot(pb, pb)
        x = x + _dot(x.astype(BF16), p.astype(BF16))
    resid = eye - (x + _dot_x3(a, x))
    return x + _dot(x.astype(BF16), resid.astype(BF16))


def _dn_chunk_body(q_ref, k_ref, v_ref, g_ref, gt_ref, u_ref, w_ref, qg_ref, kgt_ref, intra_ref, eg_ref, *, reverse):
    d = 1 if reverse else 0
    r = lax.broadcasted_iota(I32, (STEP, STEP), 0)
    c = lax.broadcasted_iota(I32, (STEP, STEP), 1)
    same = (r // DN_CHUNK) == (c // DN_CHUNK)
    incl = same & ((c >= r) if reverse else (c <= r))
    strict = same & ((c > r) if reverse else (c < r))
    eye = jnp.where(r == c, 1.0, 0.0).astype(F32)
    incl_b = jnp.where(incl, 1.0, 0.0).astype(BF16)
    incl_t_b = jnp.where(same & ((r >= c) if reverse else (r <= c)), 1.0, 0.0).astype(BF16)
    same_b = jnp.where(same, 1.0, 0.0).astype(BF16)

    gates = g_ref[0]
    gates_t = gt_ref[0]
    gh, gm, gl = _split3(gates)
    th, tm_, tl = _split3(gates_t)
    gc_col = _dot(incl_b, gh) + _dot(incl_b, gm) + _dot(incl_b, gl)
    tot_col = _dot(same_b, gh) + _dot(same_b, gm) + _dot(same_b, gl)
    gc_row = _dot(th, incl_t_b) + _dot(tm_, incl_t_b) + _dot(tl, incl_t_b)
    tot_row = _dot(th, same_b) + _dot(tm_, same_b) + _dot(tl, same_b)

    lane = lax.broadcasted_iota(I32, (STEP, 128), 1)
    eg = jnp.zeros((STEP, 128), F32)
    for h in range(DN_HEADS):
        hs = slice(h * HEAD_DIM, (h + 1) * HEAD_DIM)
        lg = d * DN_HEADS + h
        lb = 2 * DN_HEADS + d * DN_HEADS + h
        q = q_ref[0, :, hs]
        k = k_ref[0, :, hs]
        v = v_ref[0, :, hs]
        gcc = gc_col[:, lg:lg + 1]
        gcr = gc_row[lg:lg + 1, :]
        totc = tot_col[:, lg:lg + 1]
        totr = tot_row[lg:lg + 1, :]
        beta = gates[:, lb:lb + 1]
        kb = k * beta
        dec = jnp.exp(jnp.where(incl, gcc - gcr, MASK_VALUE))
        kbf = k.astype(BF16)
        a = jnp.where(strict, _dot_nt(kb.astype(BF16), kbf) * dec, 0.0)
        t = _unit_tri_inverse(a, eye)
        rhs = jnp.concatenate([v * beta, kb * jnp.exp(gcc)], axis=1)
        sol = _dot_x3(t, rhs)
        u_ref[0, :, hs] = sol[:, :HEAD_DIM]
        w_ref[0, :, hs] = sol[:, HEAD_DIM:].astype(BF16)
        intra_ref[0, :, hs] = (_dot_nt(q.astype(BF16), kbf) * dec).astype(BF16)
        qg_ref[0, :, hs] = (q * jnp.exp(gcc)).astype(BF16)
        kgt_ref[0, hs, :] = (k.T * jnp.exp(totr - gcr)).astype(BF16)
        eg = jnp.where(lane == h, jnp.exp(totc), eg)
    eg_ref[0] = eg


def _dn_chunk(dq, dk, dv, gates, gates_t, reverse):
    b, n, _ = dq.shape
    tok = lambda w: pl.BlockSpec((1, STEP, w), lambda bi, i: (bi, i, 0))
    return pl.pallas_call(
        functools.partial(_dn_chunk_body, reverse=reverse),
        grid=(b, n // STEP),
        in_specs=[tok(DN_W), tok(DN_W), tok(DN_W), tok(128),
                  pl.BlockSpec((1, 16, STEP), lambda bi, i: (bi, 0, i))],
        out_specs=[tok(DN_W), tok(DN_W), tok(DN_W),
                   pl.BlockSpec((1, DN_W, STEP), lambda bi, i: (bi, 0, i)), tok(DN_W), tok(128)],
        out_shape=[jax.ShapeDtypeStruct((b, n, DN_W), F32), jax.ShapeDtypeStruct((b, n, DN_W), BF16),
                   jax.ShapeDtypeStruct((b, n, DN_W), BF16), jax.ShapeDtypeStruct((b, DN_W, n), BF16),
                   jax.ShapeDtypeStruct((b, n, DN_W), BF16), jax.ShapeDtypeStruct((b, n, 128), F32)],
        compiler_params=_cparams(("parallel", "arbitrary"), 40 * 1024 * 1024),
        name="deltanet_chunk_bwd" if reverse else "deltanet_chunk_fwd",
    )(dq, dk, dv, gates, gates_t)


def _dn_scan_body(u_ref, w_ref, qg_ref, kgt_ref, intra_ref, eg_ref, s0_ref, o_ref, sout_ref, s_scr, *, reverse):
    i = pl.program_id(1)

    @pl.when(i == 0)
    def _():
        s_scr[...] = s0_ref[0]

    zeros = jnp.zeros((DN_CHUNK, HEAD_DIM), F32)
    order = (1, 0) if reverse else (0, 1)
    for h in range(DN_HEADS):
        hs = slice(h * HEAD_DIM, (h + 1) * HEAD_DIM)
        s = s_scr[h]
        for ci in order:
            rows = slice(ci * DN_CHUNK, (ci + 1) * DN_CHUNK)
            sb = s.astype(BF16)
            v_new = u_ref[0, rows, hs] - _dot(w_ref[0, rows, hs], sb)
            pieces = [v_new, zeros] if ci == 0 else [zeros, v_new]
            v_full = jnp.concatenate(pieces, axis=0).astype(BF16)
            o_ref[0, rows, hs] = _dot(qg_ref[0, rows, hs], sb) + _dot(intra_ref[0, rows, hs], v_full)
            e = eg_ref[0, rows, h:h + 1]
            s = s * jnp.concatenate([e, e], axis=0) + _dot(kgt_ref[0, hs, :], v_full)
        s_scr[h] = s
    sout_ref[0] = s_scr[...]


def _dn_scan(u, w, qg, kgt, intra, eg, s0, reverse):
    b, n, _ = u.shape
    ns = n // STEP
    pos = (lambda i: ns - 1 - i) if reverse else (lambda i: i)
    tok = lambda wd: pl.BlockSpec((1, STEP, wd), lambda bi, i: (bi, pos(i), 0))
    state = pl.BlockSpec((1, DN_HEADS, HEAD_DIM, HEAD_DIM), lambda bi, i: (bi, 0, 0, 0))
    return pl.pallas_call(
        functools.partial(_dn_scan_body, reverse=reverse),
        grid=(b, ns),
        in_specs=[tok(DN_W), tok(DN_W), tok(DN_W),
                  pl.BlockSpec((1, DN_W, STEP), lambda bi, i: (bi, 0, pos(i))), tok(DN_W), tok(128), state],
        out_specs=[tok(DN_W), state],
        out_shape=[jax.ShapeDtypeStruct((b, n, DN_W), F32),
                   jax.ShapeDtypeStruct((b, DN_HEADS, HEAD_DIM, HEAD_DIM), F32)],
        scratch_shapes=[pltpu.VMEM((DN_HEADS, HEAD_DIM, HEAD_DIM), F32)],
        compiler_params=_cparams(("parallel", "arbitrary"), 40 * 1024 * 1024),
        name="deltanet_scan_bwd" if reverse else "deltanet_scan_fwd",
    )(u, w, qg, kgt, intra, eg, s0)


def _deltanet(rest, restc, alog_row, dtb_row, conv_w):
    b = rest.shape[0]
    pre_c = _dn_pre(restc, alog_row, dtb_row, conv_w)
    pre_l = _dn_pre(rest, alog_row, dtb_row, conv_w)
    zero_state = jnp.zeros((b, DN_HEADS, HEAD_DIM, HEAD_DIM), F32)
    outs = {}
    for reverse in (False, True):
        oc, sc = _dn_scan(*_dn_chunk(*pre_c, reverse), zero_state, reverse)
        ol, _ = _dn_scan(*_dn_chunk(*pre_l, reverse), sc, reverse)
        outs[reverse] = (ol, oc)
    return (outs[False][0], outs[True][0]), (outs[False][1], outs[True][1])


def _outproj_body(ya_ref, yg_ref, of_ref, ob_ref, z_ref, x_ref, mod_ref, dnn_ref, w_ref, gpost_ref, gpre2_ref,
                  wr_ref, x1_ref, h2_ref, aff_ref):
    o = of_ref[0] + ob_ref[0]
    z = z_ref[0]
    acc = _dot(ya_ref[0], w_ref[0:ATTN_W, :]) + _dot(yg_ref[0], w_ref[ATTN_W:ATTN_W + GM_W, :])
    for h in range(DN_HEADS):
        hs = slice(h * HEAD_DIM, (h + 1) * HEAD_DIM)
        oh = o[:, hs]
        yh = oh * lax.rsqrt(jnp.mean(oh * oh, axis=-1, keepdims=True) + NORM_EPS) * dnn_ref[...]
        yd = (yh * _silu(z[:, hs])).astype(BF16)
        r0 = ATTN_W + GM_W + h * HEAD_DIM
        acc = acc + _dot(yd, w_ref[r0:r0 + HEAD_DIM, :])
    r = acc * lax.rsqrt(jnp.mean(acc * acc, axis=-1, keepdims=True) + NORM_EPS) * gpost_ref[...]
    x1 = x_ref[0] + mod_ref[0, 2:3, :] * r
    x1_ref[0] = x1
    y2 = x1 * lax.rsqrt(jnp.mean(x1 * x1, axis=-1, keepdims=True) + NORM_EPS) * gpre2_ref[...]
    h2 = y2 * (1.0 + mod_ref[0, 4:5, :]) + mod_ref[0, 3:4, :]
    h2_ref[0] = h2
    logits_t = _dot_x3(wr_ref[...], h2, nt=True)
    e = jnp.exp(logits_t - jnp.max(logits_t, axis=0, keepdims=True))
    aff_ref[0] = e / jnp.sum(e, axis=0, keepdims=True)


def _outproj(ya, yg, o_f, o_b, rest, x, mod, dn_norm, w_out_bf, g_post1, g_pre2, w_router_t):
    b, n, d = x.shape
    tm = 256
    mod_map = (lambda bi, i: (bi, 0, 0)) if mod.shape[0] == b else (lambda bi, i: (0, 0, 0))
    tok = lambda w: pl.BlockSpec((1, tm, w), lambda bi, i: (bi, i, 0))
    vec = lambda w: pl.BlockSpec((1, w), lambda bi, i: (0, 0))
    return pl.pallas_call(
        _outproj_body,
        grid=(b, n // tm),
        in_specs=[tok(ATTN_W), tok(GM_W), tok(DN_W), tok(DN_W),
                  pl.BlockSpec((1, tm, DN_W), lambda bi, i: (bi, i, REST_DZ // DN_W)),
                  tok(d), pl.BlockSpec((1, 6, d), mod_map), vec(HEAD_DIM),
                  pl.BlockSpec((MIX_W, d), lambda bi, i: (0, 0), pipeline_mode=pl.Buffered(1)),
                  vec(d), vec(d), pl.BlockSpec((N_EXPERTS, d), lambda bi, i: (0, 0))],
        out_specs=[tok(d), tok(d), pl.BlockSpec((1, N_EXPERTS, tm), lambda bi, i: (bi, 0, i))],
        out_shape=[jax.ShapeDtypeStruct((b, n, d), F32), jax.ShapeDtypeStruct((b, n, d), F32),
                   jax.ShapeDtypeStruct((b, N_EXPERTS, n), F32)],
        compiler_params=_cparams(("parallel", "arbitrary"), 48 * 1024 * 1024),
        name="outproj_router",
    )(ya, yg, o_f, o_b, rest, x, mod, dn_norm.reshape(1, HEAD_DIM), w_out_bf, g_post1.reshape(1, d),
      g_pre2.reshape(1, d), w_router_t)


def _ffn_body(idx_ref, h2_ref, gate_ref, w1_ref, w3_ref, w2_ref, y_ref, xs_scr, sem, *, cap, n_exp):
    e = pl.program_id(0)
    f = pl.program_id(1)

    def gather(e2, slot):
        def body(s, carry):
            t = idx_ref[e2 * cap + s]
            pltpu.make_async_copy(h2_ref.at[pl.ds(t, 1)], xs_scr.at[slot, pl.ds(s, 1)], sem.at[slot]).start()
            return carry
        lax.fori_loop(0, cap, body, 0)

    @pl.when((e == 0) & (f == 0))
    def _():
        gather(0, 0)

    @pl.when((f == 0) & (e + 1 < n_exp))
    def _():
        gather(e + 1, (e + 1) % 2)

    slot = e % 2

    @pl.when(f == 0)
    def _():
        pltpu.make_async_copy(h2_ref.at[pl.ds(0, cap)], xs_scr.at[slot], sem.at[slot]).wait()

    xb = xs_scr[slot].astype(BF16)
    a = _dot(xb, w1_ref[0])
    b3 = _dot(xb, w3_ref[0])
    act = (_silu(a) * b3).astype(BF16)
    y = _dot(act, w2_ref[0]) * gate_ref[0]

    @pl.when(f == 0)
    def _():
        y_ref[0] = y

    @pl.when(f != 0)
    def _():
        y_ref[0] += y


def _expert_ffn(idx_flat, h2b, gate, w1, w3, w2):
    n, d = h2b.shape
    n_exp, _, ff = w1.shape
    cap = idx_flat.shape[0] // n_exp
    fh = ff // 2
    grid_spec = pltpu.PrefetchScalarGridSpec(
        num_scalar_prefetch=1,
        grid=(n_exp, 2),
        in_specs=[pl.BlockSpec(memory_space=pl.ANY),
                  pl.BlockSpec((1, cap, 1), lambda e, f, idx: (e, 0, 0)),
                  pl.BlockSpec((1, d, fh), lambda e, f, idx: (e, 0, f)),
                  pl.BlockSpec((1, d, fh), lambda e, f, idx: (e, 0, f)),
                  pl.BlockSpec((1, fh, d), lambda e, f, idx: (e, f, 0))],
        out_specs=pl.BlockSpec((1, cap, d), lambda e, f, idx: (e, 0, 0)),
        scratch_shapes=[pltpu.VMEM((2, cap, d), F32), pltpu.SemaphoreType.DMA((2,))],
    )
    return pl.pallas_call(
        functools.partial(_ffn_body, cap=cap, n_exp=n_exp),
        grid_spec=grid_spec,
        out_shape=jax.ShapeDtypeStruct((n_exp, cap, d), F32),
        compiler_params=_cparams(("arbitrary", "arbitrary"), 48 * 1024 * 1024),
        name="expert_ffn",
    )(idx_flat, h2b, gate, w1, w3, w2)


def _combine_body(idx_ref, rnk_ref, seg_ref, maxc_ref, y_ref, x1_ref, mod_ref, gpost_ref, o_ref, z_scr, sem,
                  *, cap, n_exp, nt):
    tile = pl.program_id(0)
    maxc = maxc_ref[tile]
    zero_tile = jnp.zeros(o_ref.shape, F32)
    for r in range(n_exp):
        @pl.when(r < maxc)
        def _():
            z_scr[r] = zero_tile

    def row_copy(src_row, rnd, dst_row):
        return pltpu.make_async_copy(y_ref.at[pl.ds(src_row, 1)], z_scr.at[rnd, pl.ds(dst_row, 1)], sem.at[0])

    total = jnp.int32(0)
    for e in range(n_exp):
        s0 = seg_ref[e * (nt + 1) + tile]
        s1 = seg_ref[e * (nt + 1) + tile + 1]

        def body(s, carry):
            row_copy(e * cap + s, rnk_ref[e * cap + s], idx_ref[e * cap + s] - tile * STEP).start()
            return carry
        lax.fori_loop(s0, s1, body, 0)
        total = total + (s1 - s0)

    def wait_body(s, carry):
        row_copy(0, 0, 0).wait()
        return carry
    lax.fori_loop(0, total, wait_body, 0)

    o_ref[...] = zero_tile
    for r in range(n_exp):
        @pl.when(r < maxc)
        def _():
            o_ref[...] += z_scr[r]
    acc = o_ref[...]
    rr = acc * lax.rsqrt(jnp.mean(acc * acc, axis=-1, keepdims=True) + NORM_EPS) * gpost_ref[...]
    o_ref[...] = x1_ref[...] + mod_ref[5:6, :] * rr


def _combine(idx_flat, rnk_flat, seg_flat, maxc, y_flat, x1b, modb, g_post2):
    n, d = x1b.shape
    n_exp = N_EXPERTS
    cap = idx_flat.shape[0] // n_exp
    nt = n // STEP
    grid_spec = pltpu.PrefetchScalarGridSpec(
        num_scalar_prefetch=4,
        grid=(nt,),
        in_specs=[pl.BlockSpec(memory_space=pl.ANY),
                  pl.BlockSpec((STEP, d), lambda i, *_: (i, 0)),
                  pl.BlockSpec((6, d), lambda i, *_: (0, 0)),
                  pl.BlockSpec((1, d), lambda i, *_: (0, 0))],
        out_specs=pl.BlockSpec((STEP, d), lambda i, *_: (i, 0)),
        scratch_shapes=[pltpu.VMEM((n_exp, STEP, d), F32), pltpu.SemaphoreType.DMA((1,))],
    )
    return pl.pallas_call(
        functools.partial(_combine_body, cap=cap, n_exp=n_exp, nt=nt),
        grid_spec=grid_spec,
        out_shape=jax.ShapeDtypeStruct((n, d), F32),
        compiler_params=_cparams(("arbitrary",), 48 * 1024 * 1024),
        name="moe_combine",
    )(idx_flat, rnk_flat, seg_flat, maxc, y_flat, x1b, modb, g_post2.reshape(1, d))


def _route(aff_t, cap):
    b, n_exp, n = aff_t.shape
    nt = n // STEP
    gate, idx = lax.top_k(aff_t, cap)
    order = jnp.argsort(idx, axis=-1)
    idx = jnp.take_along_axis(idx, order, axis=-1).astype(I32)
    gate = jnp.take_along_axis(gate, order, axis=-1)
    sel = jnp.zeros((b, n_exp, n), I32).at[
        jnp.arange(b)[:, None, None], jnp.arange(n_exp)[None, :, None], idx].set(1)
    rank = jnp.cumsum(sel, axis=1) - sel
    rnk = jnp.take_along_axis(rank, idx, axis=-1).astype(I32)
    cnt = jnp.sum(sel, axis=1)
    maxc = jnp.max(cnt.reshape(b, nt, STEP), axis=-1).astype(I32)
    bounds = jnp.arange(nt + 1, dtype=I32) * STEP
    seg = jnp.sum(idx[:, :, None, :] < bounds[None, None, :, None], axis=-1).astype(I32)
    return idx, gate, rnk, seg, maxc


def _moe(aff_t, h2, x1, mod, g_post2, w1, w3, w2):
    b, n, d = x1.shape
    cap = EC_CAPACITY * n // N_EXPERTS
    idx, gate, rnk, seg, maxc = _route(aff_t, cap)
    outs = []
    for bi in range(b):
        idx_flat = idx[bi].reshape(-1)
        y = _expert_ffn(idx_flat, h2[bi], gate[bi][..., None], w1, w3, w2)
        modb = mod[bi] if mod.shape[0] == b else mod[0]
        outs.append(_combine(idx_flat, rnk[bi].reshape(-1), seg[bi].reshape(-1), maxc[bi],
                             y.reshape(N_EXPERTS * cap, d), x1[bi], modb, g_post2))
    return jnp.stack(outs, axis=0)


def _rope_tables(n):
    rows = n // GRID_W
    r = jnp.repeat(jnp.arange(rows, dtype=F32), GRID_W)
    col = jnp.tile(jnp.arange(GRID_W, dtype=F32), rows)
    half = HEAD_DIM // 2
    inv = ROPE_BASE ** (-jnp.arange(0, half, 2, dtype=F32) / half)
    ar, ac = r[:, None] * inv, col[:, None] * inv
    cr, sr, cc, sc = jnp.cos(ar), jnp.sin(ar), jnp.cos(ac), jnp.sin(ac)
    return (jnp.concatenate([cr, cr, cc, cc], axis=-1), jnp.concatenate([-sr, sr, -sc, sc], axis=-1))


def _prep_w_in(w_in):
    d = w_in.shape[0]
    main = w_in[:, :QKV_W + REST_GATE]
    gcols = w_in[:, QKV_W + REST_GATE:].reshape(d, 2, 2, DN_HEADS).transpose(0, 2, 1, 3).reshape(d, 4 * DN_HEADS)
    pad = jnp.zeros((d, IN_PAD_W - QKV_W - REST_GATE - 4 * DN_HEADS), w_in.dtype)
    return jnp.concatenate([main, gcols, pad], axis=1).astype(BF16)


def _lane_row(v8):
    return jnp.concatenate([v8.reshape(-1).astype(F32), jnp.zeros((128 - v8.size,), F32)]).reshape(1, 128)


def kernel(x, c, ctx, c_ctx, g_pre1, g_post1, g_pre2, g_post2, w_mod, b_mod, w_in, attn_sink, gm_ln, gm_ws, gm_bs,
           dn_conv, dn_a_log, dn_dt_bias, dn_norm, w_out, w_router, w_e1, w_e3, w_e2):
    b, n, d = x.shape
    depth = w_in.shape[0]
    rope_cs = _rope_tables(n)
    c8 = jnp.concatenate([c, c_ctx[None, :], jnp.zeros((8 - b - 1, d), F32)], axis=0)
    xc = ctx
    for l in range(depth):
        update_ctx = l < depth - 1
        mod_all = _modulation(c8, w_mod[l], b_mod[l])
        mod_lat = mod_all[:b].reshape(b, 6, d)
        mod_ctx = mod_all[b:b + 1].reshape(1, 6, d)
        w_in_bf = _prep_w_in(w_in[l])
        qkv, rest = _inproj(x, mod_lat, g_pre1[l], w_in_bf, rope_cs)
        qkvc, restc = _inproj(xc, mod_ctx, g_pre1[l], w_in_bf, None)
        ya = _attention(qkv, qkvc, attn_sink[l])
        ws_bf = gm_ws[l].astype(BF16)
        yg = _gmlp(rest, gm_ln[l], ws_bf, gm_bs[l])
        (o_f, o_b), (oc_f, oc_b) = _deltanet(rest, restc, _lane_row(dn_a_log[l]), _lane_row(dn_dt_bias[l]),
                                             dn_conv[l])
        w_out_bf = w_out[l].astype(BF16)
        w_router_t = w_router[l].T
        w1, w3, w2 = w_e1[l].astype(BF16), w_e3[l].astype(BF16), w_e2[l].astype(BF16)
        x1, h2, aff_t = _outproj(ya, yg, o_f, o_b, rest, x, mod_lat, dn_norm[l], w_out_bf, g_post1[l], g_pre2[l],
                                 w_router_t)
        x = _moe(aff_t, h2, x1, mod_lat, g_post2[l], w1, w3, w2)
        if update_ctx:
            yac = _ctx_attention(qkvc, attn_sink[l])
            ygc = _gmlp(restc, gm_ln[l], ws_bf, gm_bs[l])
            xc1, h2c, affc_t = _outproj(yac, ygc, oc_f, oc_b, restc, xc, mod_ctx, dn_norm[l], w_out_bf, g_post1[l],
                                        g_pre2[l], w_router_t)
            xc = _moe(affc_t, h2c, xc1, mod_ctx, g_post2[l], w1, w3, w2)
    return x
```

```python
import functools

import jax
import jax.numpy as jnp
from jax import lax
from jax.experimental import pallas as pl
from jax.experimental.pallas import tpu as pltpu

F32 = jnp.float32
BF16 = jnp.bfloat16
I32 = jnp.int32

HEAD_DIM = 128
GRID_W = 64
ATTN_HEADS = 8
ATTN_KV_HEADS = 2
ATTN_GROUP = ATTN_HEADS // ATTN_KV_HEADS
WINDOW = 128
ATTN_BLOCK = 128
ROPE_BASE = 10000.0
MASK_VALUE = -1e30
GM_HEADS = 4
GM_CHUNK = 128
DN_HEADS = 4
DN_CHUNK = 64
N_EXPERTS = 16
EC_CAPACITY = 2
NORM_EPS = 1e-6

ATTN_W = ATTN_HEADS * HEAD_DIM
KV_W = ATTN_KV_HEADS * HEAD_DIM
QKV_W = ATTN_W + 2 * KV_W
GM_W = GM_HEADS * HEAD_DIM
DN_W = DN_HEADS * HEAD_DIM
MIX_W = ATTN_W + GM_W + DN_W
REST_GU, REST_GV, REST_DQ, REST_DK, REST_DV, REST_DZ, REST_GATE = 0, 512, 1024, 1536, 2048, 2560, 3072
REST_W = 3200
IN_PAD_W = QKV_W + REST_W

V7X_VMEM_LIMIT = 56 * 1024 * 1024
STEP = 128


def _cparams(sem, vmem=None):
    return pltpu.CompilerParams(dimension_semantics=sem, vmem_limit_bytes=vmem)


def _silu(x):
    return x * jax.nn.sigmoid(x)


def _dot(a, b):
    return jnp.dot(a, b, preferred_element_type=F32)


def _dot_nt(a, b):
    return lax.dot_general(a, b, (((1,), (1,)), ((), ())), preferred_element_type=F32)


def _split2(a):
    hi = a.astype(BF16)
    lo = (a - hi.astype(F32)).astype(BF16)
    return hi, lo


def _split3(a):
    hi = a.astype(BF16)
    r = a - hi.astype(F32)
    mid = r.astype(BF16)
    lo = (r - mid.astype(F32)).astype(BF16)
    return hi, mid, lo


def _dot_x3(a, b, nt=False):
    d = _dot_nt if nt else _dot
    ah, al = _split2(a)
    bh, bl = _split2(b)
    return d(ah, bh) + d(ah, bl) + d(al, bh)


def _mod_body(c_ref, w_ref, b_ref, o_ref):
    s = _silu(c_ref[...]).astype(BF16)
    o_ref[...] = _dot(s, w_ref[0].astype(BF16)) + b_ref[...]


def _modulation(c8, w_mod_all, b_mod, layer):
    _, d, n6 = w_mod_all.shape
    tn = 1024
    return pl.pallas_call(
        _mod_body,
        grid=(n6 // tn,),
        in_specs=[pl.BlockSpec((8, d), lambda j: (0, 0)),
                  pl.BlockSpec((1, d, tn), lambda j: (layer, 0, j)),
                  pl.BlockSpec((1, tn), lambda j: (0, j))],
        out_specs=pl.BlockSpec((8, tn), lambda j: (0, j)),
        out_shape=jax.ShapeDtypeStruct((8, n6), F32),
        compiler_params=_cparams(("arbitrary",), 40 * 1024 * 1024),
        name="modulation",
    )(c8, w_mod_all, b_mod.reshape(1, n6))


def _inproj_body(*refs, rope, tm):
    if rope:
        x_ref, mod_ref, g_ref, w_ref, cos_ref, sin_ref, qkv_ref, rest_ref = refs
    else:
        x_ref, mod_ref, g_ref, w_ref, qkv_ref, rest_ref = refs
    x = x_ref[0]
    ms = jnp.mean(x * x, axis=-1, keepdims=True)
    y = x * lax.rsqrt(ms + NORM_EPS) * g_ref[...]
    h = y * (1.0 + mod_ref[0, 1:2, :]) + mod_ref[0, 0:1, :]
    hb = h.astype(BF16)
    if rope:
        cosf = cos_ref[...]
        sinf = sin_ref[...]
        lane = lax.broadcasted_iota(I32, (tm, HEAD_DIM), 1)
        first = (lane & 32) == 0

    def rot(t):
        if not rope:
            return t
        partner = jnp.where(first, pltpu.roll(t, 96, 1), pltpu.roll(t, 32, 1))
        return t * cosf + partner * sinf

    for c0 in (0, 512):
        t = _dot(hb, w_ref[:, c0:c0 + 512])
        for j in range(4):
            th = rot(t[:, j * 128:(j + 1) * 128]) * (HEAD_DIM ** -0.5)
            qkv_ref[0, :, c0 + j * 128:c0 + (j + 1) * 128] = th.astype(BF16)
    t = _dot(hb, w_ref[:, ATTN_W:ATTN_W + 512])
    for j in range(2):
        qkv_ref[0, :, ATTN_W + j * 128:ATTN_W + (j + 1) * 128] = rot(t[:, j * 128:(j + 1) * 128]).astype(BF16)
    qkv_ref[0, :, ATTN_W + KV_W:QKV_W] = t[:, 256:512].astype(BF16)
    for c0 in range(0, REST_W, 512):
        cw = min(512, REST_W - c0)
        rest_ref[0, :, c0:c0 + cw] = _dot(hb, w_ref[:, QKV_W + c0:QKV_W + c0 + cw])


def _inproj(x, mod, g, w_bf, rope_cs):
    b, t, d = x.shape
    tm = 256
    rope = rope_cs is not None
    mod_map = (lambda bi, i: (bi, 0, 0)) if mod.shape[0] == b else (lambda bi, i: (0, 0, 0))
    in_specs = [pl.BlockSpec((1, tm, d), lambda bi, i: (bi, i, 0)),
                pl.BlockSpec((1, 6, d), mod_map),
                pl.BlockSpec((1, d), lambda bi, i: (0, 0)),
                pl.BlockSpec((d, IN_PAD_W), lambda bi, i: (0, 0), pipeline_mode=pl.Buffered(1))]
    args = [x, mod, g.reshape(1, d), w_bf]
    if rope:
        in_specs += [pl.BlockSpec((tm, HEAD_DIM), lambda bi, i: (i, 0)),
                     pl.BlockSpec((tm, HEAD_DIM), lambda bi, i: (i, 0))]
        args += list(rope_cs)
    return pl.pallas_call(
        functools.partial(_inproj_body, rope=rope, tm=tm),
        grid=(b, t // tm),
        in_specs=in_specs,
        out_specs=[pl.BlockSpec((1, tm, QKV_W), lambda bi, i: (bi, i, 0)),
                   pl.BlockSpec((1, tm, REST_W), lambda bi, i: (bi, i, 0))],
        out_shape=[jax.ShapeDtypeStruct((b, t, QKV_W), BF16),
                   jax.ShapeDtypeStruct((b, t, REST_W), F32)],
        compiler_params=_cparams(("parallel", "arbitrary"), V7X_VMEM_LIMIT),
        name="inproj_rope" if rope else "inproj",
    )(*args)


def _group_queries(q_ref, rows, kh):
    return jnp.concatenate(
        [q_ref[0, rows, (kh * ATTN_GROUP + g) * HEAD_DIM:(kh * ATTN_GROUP + g + 1) * HEAD_DIM]
         for g in range(ATTN_GROUP)], axis=0)


def _softmax_pv_blocks(sink_ref, o_ref, blocks, scores, values):
    probs, dens = {}, {}
    for blk in blocks:
        _, kh = blk
        sink_col = jnp.concatenate(
            [jnp.full((ATTN_BLOCK, 1), sink_ref[kh * ATTN_GROUP + g], F32) for g in range(ATTN_GROUP)], axis=0)
        mx = sink_col
        for s in scores[blk]:
            mx = jnp.maximum(mx, jnp.max(s, axis=-1, keepdims=True))
        den = jnp.exp(sink_col - mx)
        es = []
        for s in scores[blk]:
            e = jnp.exp(s - mx)
            den = den + jnp.sum(e, axis=-1, keepdims=True)
            es.append(e.astype(BF16))
        probs[blk], dens[blk] = es, den
    for blk in blocks:
        j, kh = blk
        acc = None
        for e, v in zip(probs[blk], values[blk]):
            pv = _dot(e, v)
            acc = pv if acc is None else acc + pv
        o = acc / dens[blk]
        rows = slice(j * ATTN_BLOCK, (j + 1) * ATTN_BLOCK)
        for g in range(ATTN_GROUP):
            hh = kh * ATTN_GROUP + g
            o_ref[0, rows, hh * HEAD_DIM:(hh + 1) * HEAD_DIM] = o[g * ATTN_BLOCK:(g + 1) * ATTN_BLOCK].astype(BF16)


def _attn_body(sink_ref, q_ref, km_ref, vm_ref, kp_ref, vp_ref, kn_ref, vn_ref, kc_ref, vc_ref, o_ref, *, tq, n):
    i = pl.program_id(1)
    qb = tq // ATTN_BLOCK
    qi = lax.broadcasted_iota(I32, (ATTN_BLOCK, 3 * ATTN_BLOCK), 0)
    sj = lax.broadcasted_iota(I32, (ATTN_BLOCK, 3 * ATTN_BLOCK), 1) - ATTN_BLOCK

    def band(main_ref, prev_ref, next_ref, j, hs):
        pieces = []
        for blk in (j - 1, j, j + 1):
            if blk < 0:
                pieces.append(prev_ref[0, :, hs])
            elif blk >= qb:
                pieces.append(next_ref[0, :, hs])
            else:
                pieces.append(main_ref[0, blk * ATTN_BLOCK:(blk + 1) * ATTN_BLOCK, hs])
        return jnp.concatenate(pieces, axis=0)

    blocks = [(j, kh) for j in range(qb) for kh in range(ATTN_KV_HEADS)]
    scores, values = {}, {}
    for j in range(qb):
        kpos = (i * qb + j) * ATTN_BLOCK + sj
        ok = (jnp.abs(sj - qi) <= WINDOW) & (kpos >= 0) & (kpos < n)
        bias = jnp.where(ok, 0.0, MASK_VALUE).astype(F32)
        bias4 = jnp.concatenate([bias] * ATTN_GROUP, axis=0)
        rows = slice(j * ATTN_BLOCK, (j + 1) * ATTN_BLOCK)
        for kh in range(ATTN_KV_HEADS):
            hs = slice(kh * HEAD_DIM, (kh + 1) * HEAD_DIM)
            q = _group_queries(q_ref, rows, kh)
            scores[(j, kh)] = [_dot_nt(q, band(km_ref, kp_ref, kn_ref, j, hs)) + bias4,
                               _dot_nt(q, kc_ref[0, :, hs])]
            values[(j, kh)] = [band(vm_ref, vp_ref, vn_ref, j, hs), vc_ref[0, :, hs]]
    _softmax_pv_blocks(sink_ref, o_ref, blocks, scores, values)


def _attention(qkv, qkvc, sink):
    b, n, _ = qkv.shape
    m = qkvc.shape[1]
    tq = 256
    qb = tq // ATTN_BLOCK
    nb = n // ATTN_BLOCK
    kcol, vcol = ATTN_W // KV_W, ATTN_W // KV_W + 1
    smem = pl.BlockSpec(memory_space=pltpu.SMEM)
    main = lambda col: pl.BlockSpec((1, tq, KV_W), lambda bi, i: (bi, i, col))
    prev = lambda col: pl.BlockSpec((1, ATTN_BLOCK, KV_W), lambda bi, i: (bi, jnp.maximum(i * qb - 1, 0), col))
    nxt = lambda col: pl.BlockSpec((1, ATTN_BLOCK, KV_W), lambda bi, i: (bi, jnp.minimum(i * qb + qb, nb - 1), col))
    ctx = lambda col: pl.BlockSpec((1, m, KV_W), lambda bi, i: (bi, 0, col))
    return pl.pallas_call(
        functools.partial(_attn_body, tq=tq, n=n),
        grid=(b, n // tq),
        in_specs=[smem, pl.BlockSpec((1, tq, ATTN_W), lambda bi, i: (bi, i, 0)),
                  main(kcol), main(vcol), prev(kcol), prev(vcol), nxt(kcol), nxt(vcol), ctx(kcol), ctx(vcol)],
        out_specs=pl.BlockSpec((1, tq, ATTN_W), lambda bi, i: (bi, i, 0)),
        out_shape=jax.ShapeDtypeStruct((b, n, ATTN_W), BF16),
        compiler_params=_cparams(("parallel", "arbitrary"), 40 * 1024 * 1024),
        name="window_attention",
    )(sink, qkv, qkv, qkv, qkv, qkv, qkv, qkv, qkvc, qkvc)


def _ctx_attn_body(sink_ref, q_ref, kc_ref, vc_ref, o_ref, *, m):
    blocks = [(j, kh) for j in range(m // ATTN_BLOCK) for kh in range(ATTN_KV_HEADS)]
    scores, values = {}, {}
    for j, kh in blocks:
        rows = slice(j * ATTN_BLOCK, (j + 1) * ATTN_BLOCK)
        hs = slice(kh * HEAD_DIM, (kh + 1) * HEAD_DIM)
        scores[(j, kh)] = [_dot_nt(_group_queries(q_ref, rows, kh), kc_ref[0, :, hs])]
        values[(j, kh)] = [vc_ref[0, :, hs]]
    _softmax_pv_blocks(sink_ref, o_ref, blocks, scores, values)


def _ctx_attention(qkvc, sink):
    b, m, _ = qkvc.shape
    kcol, vcol = ATTN_W // KV_W, ATTN_W // KV_W + 1
    return pl.pallas_call(
        functools.partial(_ctx_attn_body, m=m),
        grid=(b,),
        in_specs=[pl.BlockSpec(memory_space=pltpu.SMEM),
                  pl.BlockSpec((1, m, ATTN_W), lambda bi: (bi, 0, 0)),
                  pl.BlockSpec((1, m, KV_W), lambda bi: (bi, 0, kcol)),
                  pl.BlockSpec((1, m, KV_W), lambda bi: (bi, 0, vcol))],
        out_specs=pl.BlockSpec((1, m, ATTN_W), lambda bi: (bi, 0, 0)),
        out_shape=jax.ShapeDtypeStruct((b, m, ATTN_W), BF16),
        compiler_params=_cparams(("parallel",), 40 * 1024 * 1024),
        name="context_attention",
    )(sink, qkvc, qkvc, qkvc)


def _gmlp_body(u_ref, v_ref, ln_ref, ws_ref, bst_ref, o_ref, *, tg):
    for ci in range(tg // GM_CHUNK):
        rows = slice(ci * GM_CHUNK, (ci + 1) * GM_CHUNK)
        u = jax.nn.gelu(u_ref[0, rows, :])
        v = jax.nn.gelu(v_ref[0, rows, :])
        for h in range(GM_HEADS):
            hs = slice(h * HEAD_DIM, (h + 1) * HEAD_DIM)
            vh = v[:, hs]
            vh = vh - jnp.mean(vh, axis=-1, keepdims=True)
            vh = vh * lax.rsqrt(jnp.mean(vh * vh, axis=-1, keepdims=True) + NORM_EPS) * ln_ref[:, hs]
            mixed = _dot(ws_ref[h], vh.astype(BF16)) + bst_ref[:, h:h + 1]
            o_ref[0, rows, hs] = (u[:, hs] * mixed).astype(BF16)


def _gmlp(rest, gm_ln, gm_ws_bf, gm_bs):
    b, n, _ = rest.shape
    tg = min(512, n)
    return pl.pallas_call(
        functools.partial(_gmlp_body, tg=tg),
        grid=(b, n // tg),
        in_specs=[pl.BlockSpec((1, tg, GM_W), lambda bi, i: (bi, i, REST_GU // GM_W)),
                  pl.BlockSpec((1, tg, GM_W), lambda bi, i: (bi, i, REST_GV // GM_W)),
                  pl.BlockSpec((1, GM_W), lambda bi, i: (0, 0)),
                  pl.BlockSpec((GM_HEADS, GM_CHUNK, GM_CHUNK), lambda bi, i: (0, 0, 0)),
                  pl.BlockSpec((GM_CHUNK, GM_HEADS), lambda bi, i: (0, 0))],
        out_specs=pl.BlockSpec((1, tg, GM_W), lambda bi, i: (bi, i, 0)),
        out_shape=jax.ShapeDtypeStruct((b, n, GM_W), BF16),
        compiler_params=_cparams(("parallel", "arbitrary"), 40 * 1024 * 1024),
        name="gmlp",
    )(rest, rest, gm_ln.reshape(1, GM_W), gm_ws_bf, gm_bs.T)


def _dn_pre_body(alog_ref, dtb_ref, cw_ref, q_ref, k_ref, v_ref, qp_ref, kp_ref, vp_ref,
                 qn_ref, kn_ref, vn_ref, gate_ref, oq_ref, ok_ref, ov_ref, g_ref, gt_ref, *, td, nt):
    i = pl.program_id(1)
    row = lax.broadcasted_iota(I32, (td, DN_W), 0)
    has_prev = (i > 0).astype(F32)
    has_next = (i < nt - 1).astype(F32)

    def conv(x_ref, p_ref, n_ref, c0):
        x = x_ref[0]
        prev_row = p_ref[0, 7:8, :] * has_prev
        next_row = n_ref[0, 0:1, :] * has_next
        xm = jnp.where(row == 0, prev_row, pltpu.roll(x, 1, 0))
        xp = jnp.where(row == td - 1, next_row, pltpu.roll(x, td - 1, 0))
        y = (cw_ref[0:1, c0:c0 + DN_W] * xm + cw_ref[1:2, c0:c0 + DN_W] * x
             + cw_ref[2:3, c0:c0 + DN_W] * xp)
        return _silu(y)

    def l2n(y, h):
        yh = y[:, h * HEAD_DIM:(h + 1) * HEAD_DIM]
        return yh * lax.rsqrt(jnp.sum(yh * yh, axis=-1, keepdims=True) + NORM_EPS)

    yq = conv(q_ref, qp_ref, qn_ref, 0)
    yk = conv(k_ref, kp_ref, kn_ref, DN_W)
    for h in range(DN_HEADS):
        hs = slice(h * HEAD_DIM, (h + 1) * HEAD_DIM)
        oq_ref[0, :, hs] = l2n(yq, h) * (HEAD_DIM ** -0.5)
        ok_ref[0, :, hs] = l2n(yk, h)
    ov_ref[0] = conv(v_ref, vp_ref, vn_ref, 2 * DN_W)

    raw = gate_ref[0]
    z = raw + dtb_ref[...]
    softplus = jnp.maximum(z, 0.0) + jnp.log(1.0 + jnp.exp(-jnp.abs(z)))
    gval = -jnp.exp(alog_ref[...]) * softplus
    lane = lax.broadcasted_iota(I32, (td, 128), 1)
    gates = jnp.where(lane < 2 * DN_HEADS, gval, jax.nn.sigmoid(raw))
    g_ref[0] = gates
    gt_ref[0] = gates.T[0:16, :]


def _dn_pre(rest, alog_row, dtb_row, conv_w):
    b, n, _ = rest.shape
    td = 256
    nt = n // td
    cur = lambda col: pl.BlockSpec((1, td, DN_W), lambda bi, i: (bi, i, col))
    prev = lambda col: pl.BlockSpec((1, 8, DN_W), lambda bi, i: (bi, jnp.maximum(i * (td // 8) - 1, 0), col))
    nxt = lambda col: pl.BlockSpec((1, 8, DN_W), lambda bi, i: (bi, jnp.minimum((i + 1) * (td // 8), n // 8 - 1), col))
    cq, ck, cv = REST_DQ // DN_W, REST_DK // DN_W, REST_DV // DN_W
    row128 = pl.BlockSpec((1, 128), lambda bi, i: (0, 0))
    tok = lambda w: pl.BlockSpec((1, td, w), lambda bi, i: (bi, i, 0))
    return pl.pallas_call(
        functools.partial(_dn_pre_body, td=td, nt=nt),
        grid=(b, nt),
        in_specs=[row128, row128, pl.BlockSpec((3, 3 * DN_W), lambda bi, i: (0, 0)),
                  cur(cq), cur(ck), cur(cv), prev(cq), prev(ck), prev(cv), nxt(cq), nxt(ck), nxt(cv),
                  pl.BlockSpec((1, td, 128), lambda bi, i: (bi, i, REST_GATE // 128))],
        out_specs=[tok(DN_W), tok(DN_W), tok(DN_W), tok(128),
                   pl.BlockSpec((1, 16, td), lambda bi, i: (bi, 0, i))],
        out_shape=[jax.ShapeDtypeStruct((b, n, DN_W), F32)] * 3
        + [jax.ShapeDtypeStruct((b, n, 128), F32), jax.ShapeDtypeStruct((b, 16, n), F32)],
        compiler_params=_cparams(("parallel", "arbitrary"), 40 * 1024 * 1024),
        name="deltanet_pre",
    )(alog_row, dtb_row, conv_w, rest, rest, rest, rest, rest, rest, rest, rest, rest, rest)


def _dn_chunk_body(q_ref, k_ref, v_ref, g_ref, gt_ref, *out_refs, nb):
    dir_refs = (out_refs[:6], out_refs[6:])
    r = lax.broadcasted_iota(I32, (STEP, STEP), 0)
    c = lax.broadcasted_iota(I32, (STEP, STEP), 1)
    same = (r // DN_CHUNK) == (c // DN_CHUNK)
    incl = (same & (c <= r), same & (c >= r))
    strict = (same & (c < r), same & (c > r))
    lower_b = jnp.where(incl[0], 1.0, 0.0).astype(BF16)
    upper_b = jnp.where(incl[1], 1.0, 0.0).astype(BF16)
    same_b = jnp.where(same, 1.0, 0.0).astype(BF16)
    lane = lax.broadcasted_iota(I32, (STEP, 128), 1)

    gates, gc_col, gc_row, tot_col, tot_row = [], [], [], [], []
    for bi in range(nb):
        g = g_ref[bi]
        gt = gt_ref[bi]
        gh, gm, gl = _split3(g)
        th, tm_, tl = _split3(gt)
        cf_col = _dot(lower_b, gh) + _dot(lower_b, gm) + _dot(lower_b, gl)
        tc = _dot(same_b, gh) + _dot(same_b, gm) + _dot(same_b, gl)
        cf_row = _dot(th, upper_b) + _dot(tm_, upper_b) + _dot(tl, upper_b)
        tr = _dot(th, same_b) + _dot(tm_, same_b) + _dot(tl, same_b)
        gates.append(g)
        gc_col.append((cf_col, tc - cf_col + g))
        gc_row.append((cf_row, tr - cf_row + gt))
        tot_col.append(tc)
        tot_row.append(tr)

    heads = [(bi, h) for bi in range(nb) for h in range(DN_HEADS)]
    hsl = lambda h: slice(h * HEAD_DIM, (h + 1) * HEAD_DIM)
    kbf = {bh: k_ref[bh[0], :, hsl(bh[1])].astype(BF16) for bh in heads}
    kk = {bh: _dot_nt(kbf[bh], kbf[bh]) for bh in heads}
    qk = {bh: _dot_nt(q_ref[bh[0], :, hsl(bh[1])].astype(BF16), kbf[bh]) for bh in heads}

    probs = [(bi, h, d) for bi, h in heads for d in range(2)]
    dec, y, p = {}, {}, {}
    for pr in probs:
        bi, h, d = pr
        lg = d * DN_HEADS + h
        lb = 2 * DN_HEADS + d * DN_HEADS + h
        gcc = gc_col[bi][d][:, lg:lg + 1]
        gcr = gc_row[bi][d][lg:lg + 1, :]
        dec[pr] = jnp.exp(jnp.where(incl[d], gcc - gcr, MASK_VALUE))
        y[pr] = -jnp.where(strict[d], kk[(bi, h)] * gates[bi][:, lb:lb + 1] * dec[pr], 0.0)
        p[pr] = y[pr]
    for _ in range(5):
        for pr in probs:
            pb = p[pr].astype(BF16)
            p[pr] = _dot(pb, pb)
        for pr in probs:
            y[pr] = y[pr] + p[pr] + _dot(y[pr].astype(BF16), p[pr].astype(BF16))

    for pr in probs:
        bi, h, d = pr
        hs = hsl(h)
        u_ref, w_ref, qg_ref, kgt_ref, intra_ref, _ = dir_refs[d]
        lg = d * DN_HEADS + h
        lb = 2 * DN_HEADS + d * DN_HEADS + h
        gcc = gc_col[bi][d][:, lg:lg + 1]
        gcr = gc_row[bi][d][lg:lg + 1, :]
        totr = tot_row[bi][lg:lg + 1, :]
        beta = gates[bi][:, lb:lb + 1]
        q = q_ref[bi, :, hs]
        k = k_ref[bi, :, hs]
        egc = jnp.exp(gcc)
        rhs = jnp.concatenate([v_ref[bi, :, hs] * beta, k * (beta * egc)], axis=1)
        sol = rhs + _dot(y[pr].astype(BF16), rhs.astype(BF16))
        u_ref[bi, :, hs] = sol[:, :HEAD_DIM]
        w_ref[bi, :, hs] = sol[:, HEAD_DIM:].astype(BF16)
        intra_ref[bi, :, hs] = (qk[(bi, h)] * dec[pr]).astype(BF16)
        qg_ref[bi, :, hs] = (q * egc).astype(BF16)
        kgt_ref[bi, hs, :] = (k.T * jnp.exp(totr - gcr)).astype(BF16)

    for bi in range(nb):
        for d in range(2):
            eg = jnp.zeros((STEP, 128), F32)
            for h in range(DN_HEADS):
                lg = d * DN_HEADS + h
                eg = jnp.where(lane == h, jnp.exp(tot_col[bi][:, lg:lg + 1]), eg)
            dir_refs[d][5][bi] = eg


DN_CHUNK_NB = 2


def _dn_chunk(dq, dk, dv, gates, gates_t):
    b, n, _ = dq.shape
    nb = DN_CHUNK_NB if b % DN_CHUNK_NB == 0 else 1
    tok = lambda w: pl.BlockSpec((nb, STEP, w), lambda bi, i: (bi, i, 0))
    one_dir_specs = [tok(DN_W), tok(DN_W), tok(DN_W),
                     pl.BlockSpec((nb, DN_W, STEP), lambda bi, i: (bi, 0, i)), tok(DN_W), tok(128)]
    one_dir_shapes = [jax.ShapeDtypeStruct((b, n, DN_W), F32), jax.ShapeDtypeStruct((b, n, DN_W), BF16),
                      jax.ShapeDtypeStruct((b, n, DN_W), BF16), jax.ShapeDtypeStruct((b, DN_W, n), BF16),
                      jax.ShapeDtypeStruct((b, n, DN_W), BF16), jax.ShapeDtypeStruct((b, n, 128), F32)]
    outs = pl.pallas_call(
        functools.partial(_dn_chunk_body, nb=nb),
        grid=(b // nb, n // STEP),
        in_specs=[tok(DN_W), tok(DN_W), tok(DN_W), tok(128),
                  pl.BlockSpec((nb, 16, STEP), lambda bi, i: (bi, 0, i))],
        out_specs=one_dir_specs * 2,
        out_shape=one_dir_shapes * 2,
        compiler_params=_cparams(("parallel", "arbitrary"), 48 * 1024 * 1024),
        name="deltanet_chunk",
    )(dq, dk, dv, gates, gates_t)
    return outs[:6], outs[6:]


def _dn_scan_body(*refs, nb, ns):
    dir_in = (refs[0:7], refs[7:14])
    o_refs = refs[14:16]
    sout_refs = refs[16:18]
    s_scr = refs[18]
    i = pl.program_id(0)

    @pl.when(i == 0)
    def _():
        s_scr[0] = dir_in[0][6][...]
        s_scr[1] = dir_in[1][6][...]

    zeros = jnp.zeros((DN_CHUNK, HEAD_DIM), F32)
    chains = [(d, bi, h) for d in range(2) for bi in range(nb) for h in range(DN_HEADS)]
    state = {ch: s_scr[ch[0], ch[1], ch[2]] for ch in chains}
    for pos in range(2):
        ws = {}
        for ch in chains:
            d, bi, h = ch
            ci = pos if d == 0 else 1 - pos
            rows = slice(ci * DN_CHUNK, (ci + 1) * DN_CHUNK)
            hs = slice(h * HEAD_DIM, (h + 1) * HEAD_DIM)
            wq = jnp.concatenate([dir_in[d][1][bi, rows, hs], dir_in[d][2][bi, rows, hs]], axis=0)
            ws[ch] = _dot(wq, state[ch].astype(BF16))
        for ch in chains:
            d, bi, h = ch
            u_ref, _, _, kgt_ref, intra_ref, eg_ref, _ = dir_in[d]
            ci = pos if d == 0 else 1 - pos
            rows = slice(ci * DN_CHUNK, (ci + 1) * DN_CHUNK)
            hs = slice(h * HEAD_DIM, (h + 1) * HEAD_DIM)
            v_new = u_ref[bi, rows, hs] - ws[ch][:DN_CHUNK]
            pieces = [v_new, zeros] if ci == 0 else [zeros, v_new]
            v_full = jnp.concatenate(pieces, axis=0).astype(BF16)
            o_refs[d][bi, rows, hs] = ws[ch][DN_CHUNK:] + _dot(intra_ref[bi, rows, hs], v_full)
            e = eg_ref[bi, rows, h:h + 1]
            state[ch] = state[ch] * jnp.concatenate([e, e], axis=0) + _dot(kgt_ref[bi, hs, :], v_full)
    for ch in chains:
        s_scr[ch[0], ch[1], ch[2]] = state[ch]

    @pl.when(i == ns - 1)
    def _():
        sout_refs[0][...] = s_scr[0]
        sout_refs[1][...] = s_scr[1]


def _dn_scan(fwd, bwd, s0_f, s0_b):
    b, n, _ = fwd[0].shape
    ns = n // STEP

    def specs(pos):
        tok = lambda wd: pl.BlockSpec((b, STEP, wd), lambda i: (0, pos(i), 0))
        return [tok(DN_W), tok(DN_W), tok(DN_W), pl.BlockSpec((b, DN_W, STEP), lambda i: (0, 0, pos(i))),
                tok(DN_W), tok(128), state]

    state = pl.BlockSpec((b, DN_HEADS, HEAD_DIM, HEAD_DIM), lambda i: (0, 0, 0, 0))
    fpos = lambda i: i
    bpos = lambda i: ns - 1 - i
    state_shape = jax.ShapeDtypeStruct((b, DN_HEADS, HEAD_DIM, HEAD_DIM), F32)
    return pl.pallas_call(
        functools.partial(_dn_scan_body, nb=b, ns=ns),
        grid=(ns,),
        in_specs=specs(fpos) + specs(bpos),
        out_specs=[pl.BlockSpec((b, STEP, DN_W), lambda i: (0, fpos(i), 0)),
                   pl.BlockSpec((b, STEP, DN_W), lambda i: (0, bpos(i), 0)), state, state],
        out_shape=[jax.ShapeDtypeStruct((b, n, DN_W), F32), jax.ShapeDtypeStruct((b, n, DN_W), F32),
                   state_shape, state_shape],
        scratch_shapes=[pltpu.VMEM((2, b, DN_HEADS, HEAD_DIM, HEAD_DIM), F32)],
        compiler_params=_cparams(("arbitrary",), 48 * 1024 * 1024),
        name="deltanet_scan",
    )(*fwd, s0_f, *bwd, s0_b)


def _deltanet(rest, restc, alog_row, dtb_row, conv_w):
    b = rest.shape[0]
    zero_state = jnp.zeros((b, DN_HEADS, HEAD_DIM, HEAD_DIM), F32)
    cf, cb = _dn_chunk(*_dn_pre(restc, alog_row, dtb_row, conv_w))
    oc_f, oc_b, sc_f, sc_b = _dn_scan(cf, cb, zero_state, zero_state)
    lf, lb = _dn_chunk(*_dn_pre(rest, alog_row, dtb_row, conv_w))
    ol_f, ol_b, _, _ = _dn_scan(lf, lb, sc_f, sc_b)
    return (ol_f, ol_b), (oc_f, oc_b)


def _outproj_body(ya_ref, yg_ref, of_ref, ob_ref, z_ref, x_ref, mod_ref, dnn_ref, w_ref, gpost_ref, gpre2_ref,
                  wr_ref, x1_ref, h2_ref, aff_ref):
    o = of_ref[0] + ob_ref[0]
    z = z_ref[0]
    acc = _dot(ya_ref[0], w_ref[0:ATTN_W, :]) + _dot(yg_ref[0], w_ref[ATTN_W:ATTN_W + GM_W, :])
    for h in range(DN_HEADS):
        hs = slice(h * HEAD_DIM, (h + 1) * HEAD_DIM)
        oh = o[:, hs]
        yh = oh * lax.rsqrt(jnp.mean(oh * oh, axis=-1, keepdims=True) + NORM_EPS) * dnn_ref[...]
        yd = (yh * _silu(z[:, hs])).astype(BF16)
        r0 = ATTN_W + GM_W + h * HEAD_DIM
        acc = acc + _dot(yd, w_ref[r0:r0 + HEAD_DIM, :])
    r = acc * lax.rsqrt(jnp.mean(acc * acc, axis=-1, keepdims=True) + NORM_EPS) * gpost_ref[...]
    x1 = x_ref[0] + mod_ref[0, 2:3, :] * r
    x1_ref[0] = x1
    y2 = x1 * lax.rsqrt(jnp.mean(x1 * x1, axis=-1, keepdims=True) + NORM_EPS) * gpre2_ref[...]
    h2 = y2 * (1.0 + mod_ref[0, 4:5, :]) + mod_ref[0, 3:4, :]
    h2_ref[0] = h2
    logits_t = _dot_x3(wr_ref[...], h2, nt=True)
    e = jnp.exp(logits_t - jnp.max(logits_t, axis=0, keepdims=True))
    aff_ref[0] = e / jnp.sum(e, axis=0, keepdims=True)


def _outproj(ya, yg, o_f, o_b, rest, x, mod, dn_norm, w_out_bf, g_post1, g_pre2, w_router_t):
    b, n, d = x.shape
    tm = 256
    mod_map = (lambda bi, i: (bi, 0, 0)) if mod.shape[0] == b else (lambda bi, i: (0, 0, 0))
    tok = lambda w: pl.BlockSpec((1, tm, w), lambda bi, i: (bi, i, 0))
    vec = lambda w: pl.BlockSpec((1, w), lambda bi, i: (0, 0))
    return pl.pallas_call(
        _outproj_body,
        grid=(b, n // tm),
        in_specs=[tok(ATTN_W), tok(GM_W), tok(DN_W), tok(DN_W),
                  pl.BlockSpec((1, tm, DN_W), lambda bi, i: (bi, i, REST_DZ // DN_W)),
                  tok(d), pl.BlockSpec((1, 6, d), mod_map), vec(HEAD_DIM),
                  pl.BlockSpec((MIX_W, d), lambda bi, i: (0, 0), pipeline_mode=pl.Buffered(1)),
                  vec(d), vec(d), pl.BlockSpec((N_EXPERTS, d), lambda bi, i: (0, 0))],
        out_specs=[tok(d), tok(d), pl.BlockSpec((1, N_EXPERTS, tm), lambda bi, i: (bi, 0, i))],
        out_shape=[jax.ShapeDtypeStruct((b, n, d), F32), jax.ShapeDtypeStruct((b, n, d), F32),
                   jax.ShapeDtypeStruct((b, N_EXPERTS, n), F32)],
        compiler_params=_cparams(("parallel", "arbitrary"), 48 * 1024 * 1024),
        name="outproj_router",
    )(ya, yg, o_f, o_b, rest, x, mod, dn_norm.reshape(1, HEAD_DIM), w_out_bf, g_post1.reshape(1, d),
      g_pre2.reshape(1, d), w_router_t)


ISSUE_UNROLL = 8
ROUND_BITS = 16


def _ffn_body(idx_ref, h2_ref, gate_ref, w1_ref, w3_ref, w2_ref, y_ref, xs_scr, sem, *, cap, n_exp, nb):
    e = pl.program_id(0)
    b = pl.program_id(1)
    t = e * nb + b
    slot = t & 1

    def gather(e2, b2, slot2):
        base = (b2 * n_exp + e2) * cap

        def body(g, carry):
            for j in range(ISSUE_UNROLL):
                s = g * ISSUE_UNROLL + j
                tok = idx_ref[base + s]
                pltpu.make_async_copy(h2_ref.at[b2, pl.ds(tok, 1)], xs_scr.at[slot2, pl.ds(s, 1)],
                                      sem.at[slot2]).start()
            return carry
        lax.fori_loop(0, cap // ISSUE_UNROLL, body, 0)

    @pl.when(t == 0)
    def _():
        gather(0, 0, 0)

    @pl.when(t + 1 < n_exp * nb)
    def _():
        wrap = b + 1 == nb
        gather(jnp.where(wrap, e + 1, e), jnp.where(wrap, 0, b + 1), 1 - slot)

    pltpu.make_async_copy(h2_ref.at[0, pl.ds(0, cap)], xs_scr.at[slot], sem.at[slot]).wait()

    rows_per = min(cap, 256)
    for r0 in range(0, cap, rows_per):
        rows = slice(r0, r0 + rows_per)
        xb = xs_scr[slot, rows, :].astype(BF16)
        act = (_silu(_dot(xb, w1_ref[0])) * _dot(xb, w3_ref[0])).astype(BF16)
        y_ref[0, 0, rows, :] = _dot(act, w2_ref[0]) * gate_ref[0, 0, rows, :]


def _expert_ffn(idx_flat, h2, gate, w1, w3, w2):
    nb, n, d = h2.shape
    n_exp, _, ff = w1.shape
    cap = idx_flat.shape[0] // (n_exp * nb)
    grid_spec = pltpu.PrefetchScalarGridSpec(
        num_scalar_prefetch=1,
        grid=(n_exp, nb),
        in_specs=[pl.BlockSpec(memory_space=pl.ANY),
                  pl.BlockSpec((1, 1, cap, 1), lambda e, b, idx: (b, e, 0, 0)),
                  pl.BlockSpec((1, d, ff), lambda e, b, idx: (e, 0, 0)),
                  pl.BlockSpec((1, d, ff), lambda e, b, idx: (e, 0, 0)),
                  pl.BlockSpec((1, ff, d), lambda e, b, idx: (e, 0, 0))],
        out_specs=pl.BlockSpec((1, 1, cap, d), lambda e, b, idx: (b, e, 0, 0)),
        scratch_shapes=[pltpu.VMEM((2, cap, d), F32), pltpu.SemaphoreType.DMA((2,))],
    )
    return pl.pallas_call(
        functools.partial(_ffn_body, cap=cap, n_exp=n_exp, nb=nb),
        grid_spec=grid_spec,
        out_shape=jax.ShapeDtypeStruct((nb, n_exp, cap, d), F32),
        compiler_params=_cparams(("arbitrary", "arbitrary"), V7X_VMEM_LIMIT),
        name="expert_ffn",
    )(idx_flat, h2, gate, w1, w3, w2)


COMBINE_COLS = 256


def _combine_body(code_ref, seg_ref, maxc_ref, y_ref, x1_ref, mod_ref, gpost_ref, o_ref, z_scr, sem,
                  *, cap, n_exp, nt, nb):
    b = pl.program_id(0)
    tile = pl.program_id(1)
    t = b * nt + tile
    slot = t & 1
    d = o_ref.shape[-1]
    zero_tile = jnp.zeros((STEP, d), F32)

    def row_copy(src_row, slot2, rnd, dst_row):
        return pltpu.make_async_copy(y_ref.at[pl.ds(src_row, 1)], z_scr.at[slot2, rnd, pl.ds(dst_row, 1)],
                                     sem.at[slot2])

    def tile_rows(b2, tile2):
        return [(seg_ref[(b2 * n_exp + e) * (nt + 1) + tile2], seg_ref[(b2 * n_exp + e) * (nt + 1) + tile2 + 1])
                for e in range(n_exp)]

    def issue(b2, tile2, slot2):
        maxc2 = maxc_ref[b2 * nt + tile2]
        for r in range(n_exp):
            @pl.when(r < maxc2)
            def _():
                z_scr[slot2, r] = zero_tile
        for e, (s0, s1) in enumerate(tile_rows(b2, tile2)):
            base = (b2 * n_exp + e) * cap

            def body(s, carry):
                code = code_ref[base + s]
                row_copy(base + s, slot2, code >> ROUND_BITS,
                         (code & ((1 << ROUND_BITS) - 1)) - tile2 * STEP).start()
                return carry
            lax.fori_loop(s0, s1, body, 0)

    @pl.when(t == 0)
    def _():
        issue(0, 0, 0)

    @pl.when(t + 1 < nb * nt)
    def _():
        wrap = tile + 1 == nt
        issue(jnp.where(wrap, b + 1, b), jnp.where(wrap, 0, tile + 1), 1 - slot)

    total = jnp.int32(0)
    for s0, s1 in tile_rows(b, tile):
        total = total + (s1 - s0)

    def wait_body(s, carry):
        row_copy(0, slot, 0, 0).wait()
        return carry
    lax.fori_loop(0, total, wait_body, 0)

    maxc = maxc_ref[t]
    ssq = jnp.zeros((STEP, 1), F32)
    for c0 in range(0, d, COMBINE_COLS):
        cols = slice(c0, c0 + COMBINE_COLS)
        acc = lax.fori_loop(0, maxc, lambda r, a: a + z_scr[slot, r, :, cols], jnp.zeros((STEP, COMBINE_COLS), F32))
        o_ref[0, :, cols] = acc
        ssq = ssq + jnp.sum(acc * acc, axis=-1, keepdims=True)
    scale = lax.rsqrt(ssq * (1.0 / d) + NORM_EPS)
    o_ref[0] = x1_ref[0] + mod_ref[0, 5:6, :] * (o_ref[0] * scale * gpost_ref[...])


def _combine(code_flat, seg_flat, maxc_flat, y_flat, x1, mod, g_post2):
    nb, n, d = x1.shape
    n_exp = N_EXPERTS
    cap = code_flat.shape[0] // (n_exp * nb)
    nt = n // STEP
    mod_map = (lambda b, i, *_: (b, 0, 0)) if mod.shape[0] == nb else (lambda b, i, *_: (0, 0, 0))
    grid_spec = pltpu.PrefetchScalarGridSpec(
        num_scalar_prefetch=3,
        grid=(nb, nt),
        in_specs=[pl.BlockSpec(memory_space=pl.ANY),
                  pl.BlockSpec((1, STEP, d), lambda b, i, *_: (b, i, 0)),
                  pl.BlockSpec((1, 6, d), mod_map),
                  pl.BlockSpec((1, d), lambda b, i, *_: (0, 0))],
        out_specs=pl.BlockSpec((1, STEP, d), lambda b, i, *_: (b, i, 0)),
        scratch_shapes=[pltpu.VMEM((2, n_exp, STEP, d), F32), pltpu.SemaphoreType.DMA((2,))],
    )
    return pl.pallas_call(
        functools.partial(_combine_body, cap=cap, n_exp=n_exp, nt=nt, nb=nb),
        grid_spec=grid_spec,
        out_shape=jax.ShapeDtypeStruct((nb, n, d), F32),
        compiler_params=_cparams(("arbitrary", "arbitrary"), V7X_VMEM_LIMIT),
        name="moe_combine",
    )(code_flat, seg_flat, maxc_flat, y_flat, x1, mod, g_post2.reshape(1, d))


def _route(aff_t, cap):
    b, n_exp, n = aff_t.shape
    nt = n // STEP
    gate, idx = lax.top_k(aff_t, cap)
    order = jnp.argsort(idx, axis=-1)
    idx = jnp.take_along_axis(idx, order, axis=-1).astype(I32)
    gate = jnp.take_along_axis(gate, order, axis=-1)
    sel = jnp.zeros((b, n_exp, n), I32).at[
        jnp.arange(b)[:, None, None], jnp.arange(n_exp)[None, :, None], idx].set(1)
    rank = jnp.cumsum(sel, axis=1) - sel
    rnk = jnp.take_along_axis(rank, idx, axis=-1).astype(I32)
    cnt = jnp.sum(sel, axis=1)
    maxc = jnp.max(cnt.reshape(b, nt, STEP), axis=-1).astype(I32)
    bounds = jnp.arange(nt + 1, dtype=I32) * STEP
    seg = jnp.sum(idx[:, :, None, :] < bounds[None, None, :, None], axis=-1).astype(I32)
    return idx, gate, rnk, seg, maxc


def _moe(aff_t, h2, x1, mod, g_post2, w1, w3, w2):
    b, n, d = x1.shape
    cap = EC_CAPACITY * n // N_EXPERTS
    idx, gate, rnk, seg, maxc = _route(aff_t, cap)
    y = _expert_ffn(idx.reshape(-1), h2, gate[..., None], w1, w3, w2)
    code = (rnk << ROUND_BITS) | idx
    return _combine(code.reshape(-1), seg.reshape(-1), maxc.reshape(-1), y.reshape(b * N_EXPERTS * cap, d),
                    x1, mod, g_post2)


def _rope_tables(n):
    rows = n // GRID_W
    r = jnp.repeat(jnp.arange(rows, dtype=F32), GRID_W)
    col = jnp.tile(jnp.arange(GRID_W, dtype=F32), rows)
    half = HEAD_DIM // 2
    inv = ROPE_BASE ** (-jnp.arange(0, half, 2, dtype=F32) / half)
    ar, ac = r[:, None] * inv, col[:, None] * inv
    cr, sr, cc, sc = jnp.cos(ar), jnp.sin(ar), jnp.cos(ac), jnp.sin(ac)
    return (jnp.concatenate([cr, cr, cc, cc], axis=-1), jnp.concatenate([-sr, sr, -sc, sc], axis=-1))


def _prep_w_in(w_in):
    d = w_in.shape[0]
    main = w_in[:, :QKV_W + REST_GATE]
    gcols = w_in[:, QKV_W + REST_GATE:].reshape(d, 2, 2, DN_HEADS).transpose(0, 2, 1, 3).reshape(d, 4 * DN_HEADS)
    pad = jnp.zeros((d, IN_PAD_W - QKV_W - REST_GATE - 4 * DN_HEADS), w_in.dtype)
    return jnp.concatenate([main, gcols, pad], axis=1).astype(BF16)


def _lane_row(v8):
    return jnp.concatenate([v8.reshape(-1).astype(F32), jnp.zeros((128 - v8.size,), F32)]).reshape(1, 128)


def kernel(x, c, ctx, c_ctx, g_pre1, g_post1, g_pre2, g_post2, w_mod, b_mod, w_in, attn_sink, gm_ln, gm_ws, gm_bs,
           dn_conv, dn_a_log, dn_dt_bias, dn_norm, w_out, w_router, w_e1, w_e3, w_e2):
    b, n, d = x.shape
    depth = w_in.shape[0]
    rope_cs = _rope_tables(n)
    c8 = jnp.concatenate([c, c_ctx[None, :], jnp.zeros((8 - b - 1, d), F32)], axis=0)
    xc = ctx
    for l in range(depth):
        update_ctx = l < depth - 1
        mod_all = _modulation(c8, w_mod, b_mod[l], l)
        mod_lat = mod_all[:b].reshape(b, 6, d)
        mod_ctx = mod_all[b:b + 1].reshape(1, 6, d)
        w_in_bf = _prep_w_in(w_in[l])
        qkv, rest = _inproj(x, mod_lat, g_pre1[l], w_in_bf, rope_cs)
        qkvc, restc = _inproj(xc, mod_ctx, g_pre1[l], w_in_bf, None)
        ya = _attention(qkv, qkvc, attn_sink[l])
        ws_bf = gm_ws[l].astype(BF16)
        yg = _gmlp(rest, gm_ln[l], ws_bf, gm_bs[l])
        (o_f, o_b), (oc_f, oc_b) = _deltanet(rest, restc, _lane_row(dn_a_log[l]), _lane_row(dn_dt_bias[l]),
                                             dn_conv[l])
        w_out_bf = w_out[l].astype(BF16)
        w_router_t = w_router[l].T
        w1, w3, w2 = w_e1[l].astype(BF16), w_e3[l].astype(BF16), w_e2[l].astype(BF16)
        x1, h2, aff_t = _outproj(ya, yg, o_f, o_b, rest, x, mod_lat, dn_norm[l], w_out_bf, g_post1[l], g_pre2[l],
                                 w_router_t)
        x = _moe(aff_t, h2, x1, mod_lat, g_post2[l], w1, w3, w2)
        if update_ctx:
            yac = _ctx_attention(qkvc, attn_sink[l])
            ygc = _gmlp(restc, gm_ln[l], ws_bf, gm_bs[l])
            xc1, h2c, affc_t = _outproj(yac, ygc, oc_f, oc_b, restc, xc, mod_ctx, dn_norm[l], w_out_bf, g_post1[l],
                                        g_pre2[l], w_router_t)
            xc = _moe(affc_t, h2c, xc1, mod_ctx, g_post2[l], w1, w3, w2)
    return x
```

```python
import functools

import jax
import jax.numpy as jnp
from jax import lax
from jax.experimental import pallas as pl
from jax.experimental.pallas import tpu as pltpu

F32 = jnp.float32
BF16 = jnp.bfloat16
I32 = jnp.int32

HEAD_DIM = 128
GRID_W = 64
ATTN_HEADS = 8
ATTN_KV_HEADS = 2
ATTN_GROUP = ATTN_HEADS // ATTN_KV_HEADS
WINDOW = 128
ATTN_BLOCK = 128
ROPE_BASE = 10000.0
MASK_VALUE = -1e30
GM_HEADS = 4
GM_CHUNK = 128
DN_HEADS = 4
DN_CHUNK = 64
N_EXPERTS = 16
EC_CAPACITY = 2
NORM_EPS = 1e-6

ATTN_W = ATTN_HEADS * HEAD_DIM
KV_W = ATTN_KV_HEADS * HEAD_DIM
QKV_W = ATTN_W + 2 * KV_W
GM_W = GM_HEADS * HEAD_DIM
DN_W = DN_HEADS * HEAD_DIM
MIX_W = ATTN_W + GM_W + DN_W
REST_GU, REST_GV, REST_DQ, REST_DK, REST_DV, REST_DZ, REST_GATE = 0, 512, 1024, 1536, 2048, 2560, 3072
REST_W = 3200
IN_PAD_W = QKV_W + REST_W

V7X_VMEM_LIMIT = 56 * 1024 * 1024
STEP = 128


def _cparams(sem, vmem=None):
    return pltpu.CompilerParams(dimension_semantics=sem, vmem_limit_bytes=vmem)


def _silu(x):
    return x * jax.nn.sigmoid(x)


def _dot(a, b):
    return jnp.dot(a, b, preferred_element_type=F32)


def _dot_nt(a, b):
    return lax.dot_general(a, b, (((1,), (1,)), ((), ())), preferred_element_type=F32)


def _split2(a):
    hi = a.astype(BF16)
    lo = (a - hi.astype(F32)).astype(BF16)
    return hi, lo


def _split3(a):
    hi = a.astype(BF16)
    r = a - hi.astype(F32)
    mid = r.astype(BF16)
    lo = (r - mid.astype(F32)).astype(BF16)
    return hi, mid, lo


def _dot_x3(a, b, nt=False):
    d = _dot_nt if nt else _dot
    ah, al = _split2(a)
    bh, bl = _split2(b)
    return d(ah, bh) + d(ah, bl) + d(al, bh)


def _mod_body(c_ref, w_ref, b_ref, o_ref):
    s = _silu(c_ref[...]).astype(BF16)
    o_ref[...] = _dot(s, w_ref[0].astype(BF16)) + b_ref[...]


def _modulation(c8, w_mod_all, b_mod, layer):
    _, d, n6 = w_mod_all.shape
    tn = 1024
    return pl.pallas_call(
        _mod_body,
        grid=(n6 // tn,),
        in_specs=[pl.BlockSpec((8, d), lambda j: (0, 0)),
                  pl.BlockSpec((1, d, tn), lambda j: (layer, 0, j)),
                  pl.BlockSpec((1, tn), lambda j: (0, j))],
        out_specs=pl.BlockSpec((8, tn), lambda j: (0, j)),
        out_shape=jax.ShapeDtypeStruct((8, n6), F32),
        compiler_params=_cparams(("arbitrary",), 40 * 1024 * 1024),
        name="modulation",
    )(c8, w_mod_all, b_mod.reshape(1, n6))


def _inproj_body(*refs, rope, tm):
    if rope:
        x_ref, mod_ref, g_ref, w_ref, cos_ref, sin_ref, qkv_ref, rest_ref = refs
    else:
        x_ref, mod_ref, g_ref, w_ref, qkv_ref, rest_ref = refs
    x = x_ref[0]
    ms = jnp.mean(x * x, axis=-1, keepdims=True)
    y = x * lax.rsqrt(ms + NORM_EPS) * g_ref[...]
    h = y * (1.0 + mod_ref[0, 1:2, :]) + mod_ref[0, 0:1, :]
    hb = h.astype(BF16)
    if rope:
        cosf = cos_ref[...]
        sinf = sin_ref[...]
        lane = lax.broadcasted_iota(I32, (tm, HEAD_DIM), 1)
        first = (lane & 32) == 0

    def rot(t):
        if not rope:
            return t
        partner = jnp.where(first, pltpu.roll(t, 96, 1), pltpu.roll(t, 32, 1))
        return t * cosf + partner * sinf

    for c0 in (0, 512):
        t = _dot(hb, w_ref[:, c0:c0 + 512])
        for j in range(4):
            th = rot(t[:, j * 128:(j + 1) * 128]) * (HEAD_DIM ** -0.5)
            qkv_ref[0, :, c0 + j * 128:c0 + (j + 1) * 128] = th.astype(BF16)
    t = _dot(hb, w_ref[:, ATTN_W:ATTN_W + 512])
    for j in range(2):
        qkv_ref[0, :, ATTN_W + j * 128:ATTN_W + (j + 1) * 128] = rot(t[:, j * 128:(j + 1) * 128]).astype(BF16)
    qkv_ref[0, :, ATTN_W + KV_W:QKV_W] = t[:, 256:512].astype(BF16)
    for c0 in range(0, REST_W, 512):
        cw = min(512, REST_W - c0)
        rest_ref[0, :, c0:c0 + cw] = _dot(hb, w_ref[:, QKV_W + c0:QKV_W + c0 + cw])


def _inproj(x, mod, g, w_bf, rope_cs):
    b, t, d = x.shape
    tm = 256
    rope = rope_cs is not None
    mod_map = (lambda bi, i: (bi, 0, 0)) if mod.shape[0] == b else (lambda bi, i: (0, 0, 0))
    in_specs = [pl.BlockSpec((1, tm, d), lambda bi, i: (bi, i, 0)),
                pl.BlockSpec((1, 6, d), mod_map),
                pl.BlockSpec((1, d), lambda bi, i: (0, 0)),
                pl.BlockSpec((d, IN_PAD_W), lambda bi, i: (0, 0), pipeline_mode=pl.Buffered(1))]
    args = [x, mod, g.reshape(1, d), w_bf]
    if rope:
        in_specs += [pl.BlockSpec((tm, HEAD_DIM), lambda bi, i: (i, 0)),
                     pl.BlockSpec((tm, HEAD_DIM), lambda bi, i: (i, 0))]
        args += list(rope_cs)
    return pl.pallas_call(
        functools.partial(_inproj_body, rope=rope, tm=tm),
        grid=(b, t // tm),
        in_specs=in_specs,
        out_specs=[pl.BlockSpec((1, tm, QKV_W), lambda bi, i: (bi, i, 0)),
                   pl.BlockSpec((1, tm, REST_W), lambda bi, i: (bi, i, 0))],
        out_shape=[jax.ShapeDtypeStruct((b, t, QKV_W), BF16),
                   jax.ShapeDtypeStruct((b, t, REST_W), F32)],
        compiler_params=_cparams(("parallel", "arbitrary"), V7X_VMEM_LIMIT),
        name="inproj_rope" if rope else "inproj",
    )(*args)


def _group_queries(q_ref, rows, kh):
    return jnp.concatenate(
        [q_ref[0, rows, (kh * ATTN_GROUP + g) * HEAD_DIM:(kh * ATTN_GROUP + g + 1) * HEAD_DIM]
         for g in range(ATTN_GROUP)], axis=0)


def _softmax_pv_blocks(sink_ref, o_ref, blocks, scores, values):
    probs, dens = {}, {}
    for blk in blocks:
        _, kh = blk
        sink_col = jnp.concatenate(
            [jnp.full((ATTN_BLOCK, 1), sink_ref[kh * ATTN_GROUP + g], F32) for g in range(ATTN_GROUP)], axis=0)
        mx = sink_col
        for s in scores[blk]:
            mx = jnp.maximum(mx, jnp.max(s, axis=-1, keepdims=True))
        probs[blk] = [jnp.exp(s - mx).astype(BF16) for s in scores[blk]]
        dens[blk] = jnp.exp(sink_col - mx)
    for blk in blocks:
        j, kh = blk
        acc = None
        for e, v in zip(probs[blk], values[blk]):
            pv = _dot(e, jnp.concatenate([v, jnp.ones_like(v)], axis=1))
            acc = pv if acc is None else acc + pv
        o = acc[:, :HEAD_DIM] / (dens[blk] + acc[:, HEAD_DIM:HEAD_DIM + 1])
        rows = slice(j * ATTN_BLOCK, (j + 1) * ATTN_BLOCK)
        for g in range(ATTN_GROUP):
            hh = kh * ATTN_GROUP + g
            o_ref[0, rows, hh * HEAD_DIM:(hh + 1) * HEAD_DIM] = o[g * ATTN_BLOCK:(g + 1) * ATTN_BLOCK].astype(BF16)


def _attn_body(sink_ref, q_ref, km_ref, vm_ref, kp_ref, vp_ref, kn_ref, vn_ref, kc_ref, vc_ref, o_ref, *, tq, n):
    i = pl.program_id(1)
    qb = tq // ATTN_BLOCK
    qi = lax.broadcasted_iota(I32, (ATTN_BLOCK, 3 * ATTN_BLOCK), 0)
    sj = lax.broadcasted_iota(I32, (ATTN_BLOCK, 3 * ATTN_BLOCK), 1) - ATTN_BLOCK

    def band(main_ref, prev_ref, next_ref, j, hs):
        pieces = []
        for blk in (j - 1, j, j + 1):
            if blk < 0:
                pieces.append(prev_ref[0, :, hs])
            elif blk >= qb:
                pieces.append(next_ref[0, :, hs])
            else:
                pieces.append(main_ref[0, blk * ATTN_BLOCK:(blk + 1) * ATTN_BLOCK, hs])
        return jnp.concatenate(pieces, axis=0)

    blocks = [(j, kh) for j in range(qb) for kh in range(ATTN_KV_HEADS)]
    scores, values = {}, {}
    for j in range(qb):
        kpos = (i * qb + j) * ATTN_BLOCK + sj
        ok = (jnp.abs(sj - qi) <= WINDOW) & (kpos >= 0) & (kpos < n)
        bias = jnp.where(ok, 0.0, MASK_VALUE).astype(F32)
        bias4 = jnp.concatenate([bias] * ATTN_GROUP, axis=0)
        rows = slice(j * ATTN_BLOCK, (j + 1) * ATTN_BLOCK)
        for kh in range(ATTN_KV_HEADS):
            hs = slice(kh * HEAD_DIM, (kh + 1) * HEAD_DIM)
            q = _group_queries(q_ref, rows, kh)
            scores[(j, kh)] = [_dot_nt(q, band(km_ref, kp_ref, kn_ref, j, hs)) + bias4,
                               _dot_nt(q, kc_ref[0, :, hs])]
            values[(j, kh)] = [band(vm_ref, vp_ref, vn_ref, j, hs), vc_ref[0, :, hs]]
    _softmax_pv_blocks(sink_ref, o_ref, blocks, scores, values)


def _attention(qkv, qkvc, sink):
    b, n, _ = qkv.shape
    m = qkvc.shape[1]
    tq = 256
    qb = tq // ATTN_BLOCK
    nb = n // ATTN_BLOCK
    kcol, vcol = ATTN_W // KV_W, ATTN_W // KV_W + 1
    smem = pl.BlockSpec(memory_space=pltpu.SMEM)
    main = lambda col: pl.BlockSpec((1, tq, KV_W), lambda bi, i: (bi, i, col))
    prev = lambda col: pl.BlockSpec((1, ATTN_BLOCK, KV_W), lambda bi, i: (bi, jnp.maximum(i * qb - 1, 0), col))
    nxt = lambda col: pl.BlockSpec((1, ATTN_BLOCK, KV_W), lambda bi, i: (bi, jnp.minimum(i * qb + qb, nb - 1), col))
    ctx = lambda col: pl.BlockSpec((1, m, KV_W), lambda bi, i: (bi, 0, col))
    return pl.pallas_call(
        functools.partial(_attn_body, tq=tq, n=n),
        grid=(b, n // tq),
        in_specs=[smem, pl.BlockSpec((1, tq, ATTN_W), lambda bi, i: (bi, i, 0)),
                  main(kcol), main(vcol), prev(kcol), prev(vcol), nxt(kcol), nxt(vcol), ctx(kcol), ctx(vcol)],
        out_specs=pl.BlockSpec((1, tq, ATTN_W), lambda bi, i: (bi, i, 0)),
        out_shape=jax.ShapeDtypeStruct((b, n, ATTN_W), BF16),
        compiler_params=_cparams(("parallel", "arbitrary"), 40 * 1024 * 1024),
        name="window_attention",
    )(sink, qkv, qkv, qkv, qkv, qkv, qkv, qkv, qkvc, qkvc)


def _ctx_attn_body(sink_ref, q_ref, kc_ref, vc_ref, o_ref, *, m):
    blocks = [(j, kh) for j in range(m // ATTN_BLOCK) for kh in range(ATTN_KV_HEADS)]
    scores, values = {}, {}
    for j, kh in blocks:
        rows = slice(j * ATTN_BLOCK, (j + 1) * ATTN_BLOCK)
        hs = slice(kh * HEAD_DIM, (kh + 1) * HEAD_DIM)
        scores[(j, kh)] = [_dot_nt(_group_queries(q_ref, rows, kh), kc_ref[0, :, hs])]
        values[(j, kh)] = [vc_ref[0, :, hs]]
    _softmax_pv_blocks(sink_ref, o_ref, blocks, scores, values)


def _ctx_attention(qkvc, sink):
    b, m, _ = qkvc.shape
    kcol, vcol = ATTN_W // KV_W, ATTN_W // KV_W + 1
    return pl.pallas_call(
        functools.partial(_ctx_attn_body, m=m),
        grid=(b,),
        in_specs=[pl.BlockSpec(memory_space=pltpu.SMEM),
                  pl.BlockSpec((1, m, ATTN_W), lambda bi: (bi, 0, 0)),
                  pl.BlockSpec((1, m, KV_W), lambda bi: (bi, 0, kcol)),
                  pl.BlockSpec((1, m, KV_W), lambda bi: (bi, 0, vcol))],
        out_specs=pl.BlockSpec((1, m, ATTN_W), lambda bi: (bi, 0, 0)),
        out_shape=jax.ShapeDtypeStruct((b, m, ATTN_W), BF16),
        compiler_params=_cparams(("parallel",), 40 * 1024 * 1024),
        name="context_attention",
    )(sink, qkvc, qkvc, qkvc)


def _gmlp_body(u_ref, v_ref, ln_ref, ws_ref, bst_ref, o_ref, *, tg):
    for ci in range(tg // GM_CHUNK):
        rows = slice(ci * GM_CHUNK, (ci + 1) * GM_CHUNK)
        u = jax.nn.gelu(u_ref[0, rows, :])
        v = jax.nn.gelu(v_ref[0, rows, :])
        for h in range(GM_HEADS):
            hs = slice(h * HEAD_DIM, (h + 1) * HEAD_DIM)
            vh = v[:, hs]
            vh = vh - jnp.mean(vh, axis=-1, keepdims=True)
            vh = vh * lax.rsqrt(jnp.mean(vh * vh, axis=-1, keepdims=True) + NORM_EPS) * ln_ref[:, hs]
            mixed = _dot(ws_ref[h], vh.astype(BF16)) + bst_ref[:, h:h + 1]
            o_ref[0, rows, hs] = (u[:, hs] * mixed).astype(BF16)


def _gmlp(rest, gm_ln, gm_ws_bf, gm_bs):
    b, n, _ = rest.shape
    tg = min(512, n)
    return pl.pallas_call(
        functools.partial(_gmlp_body, tg=tg),
        grid=(b, n // tg),
        in_specs=[pl.BlockSpec((1, tg, GM_W), lambda bi, i: (bi, i, REST_GU // GM_W)),
                  pl.BlockSpec((1, tg, GM_W), lambda bi, i: (bi, i, REST_GV // GM_W)),
                  pl.BlockSpec((1, GM_W), lambda bi, i: (0, 0)),
                  pl.BlockSpec((GM_HEADS, GM_CHUNK, GM_CHUNK), lambda bi, i: (0, 0, 0)),
                  pl.BlockSpec((GM_CHUNK, GM_HEADS), lambda bi, i: (0, 0))],
        out_specs=pl.BlockSpec((1, tg, GM_W), lambda bi, i: (bi, i, 0)),
        out_shape=jax.ShapeDtypeStruct((b, n, GM_W), BF16),
        compiler_params=_cparams(("parallel", "arbitrary"), 40 * 1024 * 1024),
        name="gmlp",
    )(rest, rest, gm_ln.reshape(1, GM_W), gm_ws_bf, gm_bs.T)


def _dn_pre_body(alog_ref, dtb_ref, cw_ref, q_ref, k_ref, v_ref, qp_ref, kp_ref, vp_ref,
                 qn_ref, kn_ref, vn_ref, gate_ref, oq_ref, ok_ref, ov_ref, g_ref, gt_ref, *, td, nt):
    i = pl.program_id(1)
    row = lax.broadcasted_iota(I32, (td, DN_W), 0)
    has_prev = (i > 0).astype(F32)
    has_next = (i < nt - 1).astype(F32)

    def conv(x_ref, p_ref, n_ref, c0):
        x = x_ref[0]
        prev_row = p_ref[0, 7:8, :] * has_prev
        next_row = n_ref[0, 0:1, :] * has_next
        xm = jnp.where(row == 0, prev_row, pltpu.roll(x, 1, 0))
        xp = jnp.where(row == td - 1, next_row, pltpu.roll(x, td - 1, 0))
        y = (cw_ref[0:1, c0:c0 + DN_W] * xm + cw_ref[1:2, c0:c0 + DN_W] * x
             + cw_ref[2:3, c0:c0 + DN_W] * xp)
        return _silu(y)

    def l2n(y, h):
        yh = y[:, h * HEAD_DIM:(h + 1) * HEAD_DIM]
        return yh * lax.rsqrt(jnp.sum(yh * yh, axis=-1, keepdims=True) + NORM_EPS)

    yq = conv(q_ref, qp_ref, qn_ref, 0)
    yk = conv(k_ref, kp_ref, kn_ref, DN_W)
    for h in range(DN_HEADS):
        hs = slice(h * HEAD_DIM, (h + 1) * HEAD_DIM)
        oq_ref[0, :, hs] = l2n(yq, h) * (HEAD_DIM ** -0.5)
        ok_ref[0, :, hs] = l2n(yk, h)
    ov_ref[0] = conv(v_ref, vp_ref, vn_ref, 2 * DN_W)

    raw = gate_ref[0]
    z = raw + dtb_ref[...]
    softplus = jnp.maximum(z, 0.0) + jnp.log(1.0 + jnp.exp(-jnp.abs(z)))
    gval = -jnp.exp(alog_ref[...]) * softplus
    lane = lax.broadcasted_iota(I32, (td, 128), 1)
    gates = jnp.where(lane < 2 * DN_HEADS, gval, jax.nn.sigmoid(raw))
    g_ref[0] = gates
    gt_ref[0] = gates.T[0:16, :]


def _dn_pre(rest, alog_row, dtb_row, conv_w):
    b, n, _ = rest.shape
    td = 256
    nt = n // td
    cur = lambda col: pl.BlockSpec((1, td, DN_W), lambda bi, i: (bi, i, col))
    prev = lambda col: pl.BlockSpec((1, 8, DN_W), lambda bi, i: (bi, jnp.maximum(i * (td // 8) - 1, 0), col))
    nxt = lambda col: pl.BlockSpec((1, 8, DN_W), lambda bi, i: (bi, jnp.minimum((i + 1) * (td // 8), n // 8 - 1), col))
    cq, ck, cv = REST_DQ // DN_W, REST_DK // DN_W, REST_DV // DN_W
    row128 = pl.BlockSpec((1, 128), lambda bi, i: (0, 0))
    tok = lambda w: pl.BlockSpec((1, td, w), lambda bi, i: (bi, i, 0))
    return pl.pallas_call(
        functools.partial(_dn_pre_body, td=td, nt=nt),
        grid=(b, nt),
        in_specs=[row128, row128, pl.BlockSpec((3, 3 * DN_W), lambda bi, i: (0, 0)),
                  cur(cq), cur(ck), cur(cv), prev(cq), prev(ck), prev(cv), nxt(cq), nxt(ck), nxt(cv),
                  pl.BlockSpec((1, td, 128), lambda bi, i: (bi, i, REST_GATE // 128))],
        out_specs=[tok(DN_W), tok(DN_W), tok(DN_W), tok(128),
                   pl.BlockSpec((1, 16, td), lambda bi, i: (bi, 0, i))],
        out_shape=[jax.ShapeDtypeStruct((b, n, DN_W), F32)] * 3
        + [jax.ShapeDtypeStruct((b, n, 128), F32), jax.ShapeDtypeStruct((b, 16, n), F32)],
        compiler_params=_cparams(("parallel", "arbitrary"), 40 * 1024 * 1024),
        name="deltanet_pre",
    )(alog_row, dtb_row, conv_w, rest, rest, rest, rest, rest, rest, rest, rest, rest, rest)


def _dn_chunk_body(q_ref, k_ref, v_ref, g_ref, gt_ref, *out_refs, nb):
    dir_refs = (out_refs[:6], out_refs[6:])
    r = lax.broadcasted_iota(I32, (STEP, STEP), 0)
    c = lax.broadcasted_iota(I32, (STEP, STEP), 1)
    same = (r // DN_CHUNK) == (c // DN_CHUNK)
    incl = (same & (c <= r), same & (c >= r))
    strict = (same & (c < r), same & (c > r))
    lower_b = jnp.where(incl[0], 1.0, 0.0).astype(BF16)
    upper_b = jnp.where(incl[1], 1.0, 0.0).astype(BF16)
    same_b = jnp.where(same, 1.0, 0.0).astype(BF16)
    lane = lax.broadcasted_iota(I32, (STEP, 128), 1)

    gates, gc_col, gc_row, tot_col, tot_row = [], [], [], [], []
    for bi in range(nb):
        g = g_ref[bi]
        gt = gt_ref[bi]
        gh, gm, gl = _split3(g)
        th, tm_, tl = _split3(gt)
        cf_col = _dot(lower_b, gh) + _dot(lower_b, gm) + _dot(lower_b, gl)
        tc = _dot(same_b, gh) + _dot(same_b, gm) + _dot(same_b, gl)
        cf_row = _dot(th, upper_b) + _dot(tm_, upper_b) + _dot(tl, upper_b)
        tr = _dot(th, same_b) + _dot(tm_, same_b) + _dot(tl, same_b)
        gates.append(g)
        gc_col.append((cf_col, tc - cf_col + g))
        gc_row.append((cf_row, tr - cf_row + gt))
        tot_col.append(tc)
        tot_row.append(tr)

    heads = [(bi, h) for bi in range(nb) for h in range(DN_HEADS)]
    hsl = lambda h: slice(h * HEAD_DIM, (h + 1) * HEAD_DIM)
    kbf = {bh: k_ref[bh[0], :, hsl(bh[1])].astype(BF16) for bh in heads}
    kk = {bh: _dot_nt(kbf[bh], kbf[bh]) for bh in heads}
    qk = {bh: _dot_nt(q_ref[bh[0], :, hsl(bh[1])].astype(BF16), kbf[bh]) for bh in heads}

    probs = [(bi, h, d) for bi, h in heads for d in range(2)]
    dec, y, p = {}, {}, {}
    for pr in probs:
        bi, h, d = pr
        lg = d * DN_HEADS + h
        lb = 2 * DN_HEADS + d * DN_HEADS + h
        gcc = gc_col[bi][d][:, lg:lg + 1]
        gcr = gc_row[bi][d][lg:lg + 1, :]
        dec[pr] = jnp.exp(jnp.where(incl[d], gcc - gcr, MASK_VALUE))
        y[pr] = -jnp.where(strict[d], kk[(bi, h)] * gates[bi][:, lb:lb + 1] * dec[pr], 0.0)
        p[pr] = y[pr]
    for _ in range(5):
        for pr in probs:
            pb = p[pr].astype(BF16)
            p[pr] = _dot(pb, pb)
        for pr in probs:
            y[pr] = y[pr] + p[pr] + _dot(y[pr].astype(BF16), p[pr].astype(BF16))

    for pr in probs:
        bi, h, d = pr
        hs = hsl(h)
        u_ref, w_ref, qg_ref, kgt_ref, intra_ref, _ = dir_refs[d]
        lg = d * DN_HEADS + h
        lb = 2 * DN_HEADS + d * DN_HEADS + h
        gcc = gc_col[bi][d][:, lg:lg + 1]
        gcr = gc_row[bi][d][lg:lg + 1, :]
        totr = tot_row[bi][lg:lg + 1, :]
        beta = gates[bi][:, lb:lb + 1]
        q = q_ref[bi, :, hs]
        k = k_ref[bi, :, hs]
        egc = jnp.exp(gcc)
        rhs = jnp.concatenate([v_ref[bi, :, hs] * beta, k * (beta * egc)], axis=1)
        sol = rhs + _dot(y[pr].astype(BF16), rhs.astype(BF16))
        u_ref[bi, :, hs] = sol[:, :HEAD_DIM]
        w_ref[bi, :, hs] = sol[:, HEAD_DIM:].astype(BF16)
        intra_ref[bi, :, hs] = (qk[(bi, h)] * dec[pr]).astype(BF16)
        qg_ref[bi, :, hs] = (q * egc).astype(BF16)
        kgt_ref[bi, hs, :] = (k.T * jnp.exp(totr - gcr)).astype(BF16)

    for bi in range(nb):
        for d in range(2):
            eg = jnp.zeros((STEP, 128), F32)
            for h in range(DN_HEADS):
                lg = d * DN_HEADS + h
                eg = jnp.where(lane == h, jnp.exp(tot_col[bi][:, lg:lg + 1]), eg)
            dir_refs[d][5][bi] = eg


DN_CHUNK_NB = 2


def _dn_chunk(dq, dk, dv, gates, gates_t):
    b, n, _ = dq.shape
    nb = DN_CHUNK_NB if b % DN_CHUNK_NB == 0 else 1
    tok = lambda w: pl.BlockSpec((nb, STEP, w), lambda bi, i: (bi, i, 0))
    one_dir_specs = [tok(DN_W), tok(DN_W), tok(DN_W),
                     pl.BlockSpec((nb, DN_W, STEP), lambda bi, i: (bi, 0, i)), tok(DN_W), tok(128)]
    one_dir_shapes = [jax.ShapeDtypeStruct((b, n, DN_W), F32), jax.ShapeDtypeStruct((b, n, DN_W), BF16),
                      jax.ShapeDtypeStruct((b, n, DN_W), BF16), jax.ShapeDtypeStruct((b, DN_W, n), BF16),
                      jax.ShapeDtypeStruct((b, n, DN_W), BF16), jax.ShapeDtypeStruct((b, n, 128), F32)]
    outs = pl.pallas_call(
        functools.partial(_dn_chunk_body, nb=nb),
        grid=(b // nb, n // STEP),
        in_specs=[tok(DN_W), tok(DN_W), tok(DN_W), tok(128),
                  pl.BlockSpec((nb, 16, STEP), lambda bi, i: (bi, 0, i))],
        out_specs=one_dir_specs * 2,
        out_shape=one_dir_shapes * 2,
        compiler_params=_cparams(("parallel", "arbitrary"), 48 * 1024 * 1024),
        name="deltanet_chunk",
    )(dq, dk, dv, gates, gates_t)
    return outs[:6], outs[6:]


def _dn_scan_body(*refs, nb, ns):
    dir_in = (refs[0:7], refs[7:14])
    o_refs = refs[14:16]
    sout_refs = refs[16:18]
    s_scr = refs[18]
    i = pl.program_id(0)

    @pl.when(i == 0)
    def _():
        s_scr[0] = dir_in[0][6][...]
        s_scr[1] = dir_in[1][6][...]

    zeros = jnp.zeros((DN_CHUNK, HEAD_DIM), F32)
    chains = [(d, bi, h) for d in range(2) for bi in range(nb) for h in range(DN_HEADS)]
    state = {ch: s_scr[ch[0], ch[1], ch[2]] for ch in chains}
    for pos in range(2):
        ws = {}
        for ch in chains:
            d, bi, h = ch
            ci = pos if d == 0 else 1 - pos
            rows = slice(ci * DN_CHUNK, (ci + 1) * DN_CHUNK)
            hs = slice(h * HEAD_DIM, (h + 1) * HEAD_DIM)
            wq = jnp.concatenate([dir_in[d][1][bi, rows, hs], dir_in[d][2][bi, rows, hs]], axis=0)
            ws[ch] = _dot(wq, state[ch].astype(BF16))
        for ch in chains:
            d, bi, h = ch
            u_ref, _, _, kgt_ref, intra_ref, eg_ref, _ = dir_in[d]
            ci = pos if d == 0 else 1 - pos
            rows = slice(ci * DN_CHUNK, (ci + 1) * DN_CHUNK)
            hs = slice(h * HEAD_DIM, (h + 1) * HEAD_DIM)
            v_new = u_ref[bi, rows, hs] - ws[ch][:DN_CHUNK]
            pieces = [v_new, zeros] if ci == 0 else [zeros, v_new]
            v_full = jnp.concatenate(pieces, axis=0).astype(BF16)
            o_refs[d][bi, rows, hs] = ws[ch][DN_CHUNK:] + _dot(intra_ref[bi, rows, hs], v_full)
            e = eg_ref[bi, rows, h:h + 1]
            state[ch] = state[ch] * jnp.concatenate([e, e], axis=0) + _dot(kgt_ref[bi, hs, :], v_full)
    for ch in chains:
        s_scr[ch[0], ch[1], ch[2]] = state[ch]

    @pl.when(i == ns - 1)
    def _():
        sout_refs[0][...] = s_scr[0]
        sout_refs[1][...] = s_scr[1]


def _dn_scan(fwd, bwd, s0_f, s0_b):
    b, n, _ = fwd[0].shape
    ns = n // STEP

    def specs(pos):
        tok = lambda wd: pl.BlockSpec((b, STEP, wd), lambda i: (0, pos(i), 0))
        return [tok(DN_W), tok(DN_W), tok(DN_W), pl.BlockSpec((b, DN_W, STEP), lambda i: (0, 0, pos(i))),
                tok(DN_W), tok(128), state]

    state = pl.BlockSpec((b, DN_HEADS, HEAD_DIM, HEAD_DIM), lambda i: (0, 0, 0, 0))
    fpos = lambda i: i
    bpos = lambda i: ns - 1 - i
    state_shape = jax.ShapeDtypeStruct((b, DN_HEADS, HEAD_DIM, HEAD_DIM), F32)
    return pl.pallas_call(
        functools.partial(_dn_scan_body, nb=b, ns=ns),
        grid=(ns,),
        in_specs=specs(fpos) + specs(bpos),
        out_specs=[pl.BlockSpec((b, STEP, DN_W), lambda i: (0, fpos(i), 0)),
                   pl.BlockSpec((b, STEP, DN_W), lambda i: (0, bpos(i), 0)), state, state],
        out_shape=[jax.ShapeDtypeStruct((b, n, DN_W), F32), jax.ShapeDtypeStruct((b, n, DN_W), F32),
                   state_shape, state_shape],
        scratch_shapes=[pltpu.VMEM((2, b, DN_HEADS, HEAD_DIM, HEAD_DIM), F32)],
        compiler_params=_cparams(("arbitrary",), 48 * 1024 * 1024),
        name="deltanet_scan",
    )(*fwd, s0_f, *bwd, s0_b)


def _deltanet(rest, restc, alog_row, dtb_row, conv_w):
    b = rest.shape[0]
    zero_state = jnp.zeros((b, DN_HEADS, HEAD_DIM, HEAD_DIM), F32)
    cf, cb = _dn_chunk(*_dn_pre(restc, alog_row, dtb_row, conv_w))
    oc_f, oc_b, sc_f, sc_b = _dn_scan(cf, cb, zero_state, zero_state)
    lf, lb = _dn_chunk(*_dn_pre(rest, alog_row, dtb_row, conv_w))
    ol_f, ol_b, _, _ = _dn_scan(lf, lb, sc_f, sc_b)
    return (ol_f, ol_b), (oc_f, oc_b)


OUTPROJ_SUB = 128


def _outproj_body(ya_ref, yg_ref, of_ref, ob_ref, z_ref, x_ref, mod_ref, dnn_ref, w_ref, gpost_ref, gpre2_ref,
                  wr_ref, x1_ref, h2_ref, aff_ref, *, tm):
    def project(rows):
        o = of_ref[0, rows, :] + ob_ref[0, rows, :]
        z = z_ref[0, rows, :]
        acc = _dot(ya_ref[0, rows, :], w_ref[0:ATTN_W, :]) + _dot(yg_ref[0, rows, :], w_ref[ATTN_W:ATTN_W + GM_W, :])
        for h in range(DN_HEADS):
            hs = slice(h * HEAD_DIM, (h + 1) * HEAD_DIM)
            oh = o[:, hs]
            yh = oh * lax.rsqrt(jnp.mean(oh * oh, axis=-1, keepdims=True) + NORM_EPS) * dnn_ref[...]
            yd = (yh * _silu(z[:, hs])).astype(BF16)
            r0 = ATTN_W + GM_W + h * HEAD_DIM
            acc = acc + _dot(yd, w_ref[r0:r0 + HEAD_DIM, :])
        return acc

    def finish(rows, acc):
        r = acc * lax.rsqrt(jnp.mean(acc * acc, axis=-1, keepdims=True) + NORM_EPS) * gpost_ref[...]
        x1 = x_ref[0, rows, :] + mod_ref[0, 2:3, :] * r
        x1_ref[0, rows, :] = x1
        y2 = x1 * lax.rsqrt(jnp.mean(x1 * x1, axis=-1, keepdims=True) + NORM_EPS) * gpre2_ref[...]
        h2 = y2 * (1.0 + mod_ref[0, 4:5, :]) + mod_ref[0, 3:4, :]
        h2_ref[0, rows, :] = h2
        logits_t = _dot_x3(wr_ref[...], h2, nt=True)
        e = jnp.exp(logits_t - jnp.max(logits_t, axis=0, keepdims=True))
        aff_ref[0, :, rows] = e / jnp.sum(e, axis=0, keepdims=True)

    sub = min(OUTPROJ_SUB, tm)
    blocks = [slice(r0, r0 + sub) for r0 in range(0, tm, sub)]
    pending = None
    for rows in blocks:
        acc = project(rows)
        if pending is not None:
            finish(*pending)
        pending = (rows, acc)
    finish(*pending)


def _outproj(ya, yg, o_f, o_b, rest, x, mod, dn_norm, w_out_bf, g_post1, g_pre2, w_router_t):
    b, n, d = x.shape
    tm = min(512, n)
    mod_map = (lambda bi, i: (bi, 0, 0)) if mod.shape[0] == b else (lambda bi, i: (0, 0, 0))
    tok = lambda w: pl.BlockSpec((1, tm, w), lambda bi, i: (bi, i, 0))
    vec = lambda w: pl.BlockSpec((1, w), lambda bi, i: (0, 0))
    return pl.pallas_call(
        functools.partial(_outproj_body, tm=tm),
        grid=(b, n // tm),
        in_specs=[tok(ATTN_W), tok(GM_W), tok(DN_W), tok(DN_W),
                  pl.BlockSpec((1, tm, DN_W), lambda bi, i: (bi, i, REST_DZ // DN_W)),
                  tok(d), pl.BlockSpec((1, 6, d), mod_map), vec(HEAD_DIM),
                  pl.BlockSpec((MIX_W, d), lambda bi, i: (0, 0), pipeline_mode=pl.Buffered(1)),
                  vec(d), vec(d), pl.BlockSpec((N_EXPERTS, d), lambda bi, i: (0, 0))],
        out_specs=[tok(d), tok(d), pl.BlockSpec((1, N_EXPERTS, tm), lambda bi, i: (bi, 0, i))],
        out_shape=[jax.ShapeDtypeStruct((b, n, d), F32), jax.ShapeDtypeStruct((b, n, d), F32),
                   jax.ShapeDtypeStruct((b, N_EXPERTS, n), F32)],
        compiler_params=_cparams(("parallel", "arbitrary"), V7X_VMEM_LIMIT),
        name="outproj_router",
    )(ya, yg, o_f, o_b, rest, x, mod, dn_norm.reshape(1, HEAD_DIM), w_out_bf, g_post1.reshape(1, d),
      g_pre2.reshape(1, d), w_router_t)


SUBLANES = 8
TOK_BITS = 12


def _ffn_body(code_ref, h2_ref, gate_ref, w1_ref, w3_ref, w2_ref, ytm_ref, xs_scr, ys_scr, gsem, ssem,
              *, cap, n_exp, nb):
    e = pl.program_id(0)
    b = pl.program_id(1)
    t = e * nb + b
    slot = t & 1
    groups = cap // SUBLANES
    last = n_exp * nb - 1

    def gather(e2, b2, slot2):
        base = (b2 * n_exp + e2) * cap

        def body(g, carry):
            for j in range(SUBLANES):
                tok = code_ref[base + g * SUBLANES + j] & ((1 << TOK_BITS) - 1)
                pltpu.make_async_copy(h2_ref.at[b2, tok >> 3, pl.ds(tok & 7, 1)],
                                      xs_scr.at[slot2, g, pl.ds(j, 1)], gsem.at[slot2]).start()
            return carry
        lax.fori_loop(0, groups, body, 0)

    def scatter_wait(slot2):
        pltpu.make_async_copy(ys_scr.at[slot2], ytm_ref.at[pl.ds(0, groups)], ssem.at[slot2]).wait()

    @pl.when(t == 0)
    def _():
        gather(0, 0, 0)

    @pl.when(t < last)
    def _():
        wrap = b + 1 == nb
        gather(jnp.where(wrap, e + 1, e), jnp.where(wrap, 0, b + 1), 1 - slot)

    pltpu.make_async_copy(h2_ref.at[0, pl.ds(0, groups)], xs_scr.at[slot], gsem.at[slot]).wait()

    @pl.when(t >= 2)
    def _():
        scatter_wait(slot)

    gpr = min(groups, 256 // SUBLANES)
    for g0 in range(0, groups, gpr):
        rows = gpr * SUBLANES
        xb = xs_scr[slot, g0:g0 + gpr].reshape(rows, -1).astype(BF16)
        act = (_silu(_dot(xb, w1_ref[0])) * _dot(xb, w3_ref[0])).astype(BF16)
        y = _dot(act, w2_ref[0]) * gate_ref[0, 0, g0 * SUBLANES:g0 * SUBLANES + rows, :]
        ys_scr[slot, g0:g0 + gpr] = y.reshape(gpr, SUBLANES, -1)

    base = (b * n_exp + e) * cap
    row0 = b * n_exp * cap

    def scatter_body(g, carry):
        for j in range(SUBLANES):
            p = row0 + (code_ref[base + g * SUBLANES + j] >> TOK_BITS)
            pltpu.make_async_copy(ys_scr.at[slot, g, pl.ds(j, 1)], ytm_ref.at[p >> 3, pl.ds(p & 7, 1)],
                                  ssem.at[slot]).start()
        return carry
    lax.fori_loop(0, groups, scatter_body, 0)

    @pl.when(t == last)
    def _():
        if last >= 1:
            scatter_wait(1 - slot)
        scatter_wait(slot)


def _expert_ffn(code_flat, h2, gate, w1, w3, w2):
    nb, n, d = h2.shape
    n_exp, _, ff = w1.shape
    cap = code_flat.shape[0] // (n_exp * nb)
    groups = cap // SUBLANES
    grid_spec = pltpu.PrefetchScalarGridSpec(
        num_scalar_prefetch=1,
        grid=(n_exp, nb),
        in_specs=[pl.BlockSpec(memory_space=pl.ANY),
                  pl.BlockSpec((1, 1, cap, 1), lambda e, b, code: (b, e, 0, 0)),
                  pl.BlockSpec((1, d, ff), lambda e, b, code: (e, 0, 0)),
                  pl.BlockSpec((1, d, ff), lambda e, b, code: (e, 0, 0)),
                  pl.BlockSpec((1, ff, d), lambda e, b, code: (e, 0, 0))],
        out_specs=pl.BlockSpec(memory_space=pl.ANY),
        scratch_shapes=[pltpu.VMEM((2, groups, SUBLANES, d), F32), pltpu.VMEM((2, groups, SUBLANES, d), F32),
                        pltpu.SemaphoreType.DMA((2,)), pltpu.SemaphoreType.DMA((2,))],
    )
    return pl.pallas_call(
        functools.partial(_ffn_body, cap=cap, n_exp=n_exp, nb=nb),
        grid_spec=grid_spec,
        out_shape=jax.ShapeDtypeStruct((nb * n_exp * groups, SUBLANES, d), F32),
        compiler_params=_cparams(("arbitrary", "arbitrary"), V7X_VMEM_LIMIT),
        name="expert_ffn",
    )(code_flat, h2.reshape(nb, n // SUBLANES, SUBLANES, d), gate, w1, w3, w2)


def _combine_body(off_ref, ytm_ref, tok_ref, x1_ref, mod_ref, gpost_ref, o_ref, ybuf, sem, *, rows_pb, nt, nb):
    b = pl.program_id(0)
    tile = pl.program_id(1)
    t = b * nt + tile
    d = o_ref.shape[-1]

    def chunk_range(b2, tile2):
        o0 = off_ref[b2 * (nt + 1) + tile2]
        o1 = off_ref[b2 * (nt + 1) + tile2 + 1]
        c0 = o0 >> 7
        return c0, jnp.where(o1 > o0, ((o1 + STEP - 1) >> 7) - c0, 0)

    def chunk_copy(b2, c, slot):
        return pltpu.make_async_copy(ytm_ref.at[pl.ds(b2 * rows_pb + c * STEP, STEP)], ybuf.at[slot], sem.at[slot])

    c0, nchunks = chunk_range(b, tile)

    @pl.when((t == 0) & (nchunks > 0))
    def _():
        chunk_copy(b, c0, 0).start()

    o_ref[0] = jnp.zeros((STEP, d), F32)
    token = tile * STEP + lax.broadcasted_iota(I32, (STEP, STEP), 0)

    def body(i, carry):
        slot = i & 1
        c = c0 + i

        @pl.when(i + 1 < nchunks)
        def _():
            chunk_copy(b, c + 1, 1 - slot).start()

        chunk_copy(b, c, slot).wait()
        onehot = jnp.where(tok_ref[0, pl.ds(c, 1), :] == token, 1.0, 0.0).astype(BF16)
        y = ybuf[slot]
        hi = y.astype(BF16)
        lo = (y - hi.astype(F32)).astype(BF16)
        o_ref[0] += _dot(onehot, hi) + _dot(onehot, lo)
        return carry
    lax.fori_loop(0, nchunks, body, 0)

    @pl.when(t + 1 < nb * nt)
    def _():
        wrap = tile + 1 == nt
        b2 = jnp.where(wrap, b + 1, b)
        c0n, nn = chunk_range(b2, jnp.where(wrap, 0, tile + 1))

        @pl.when(nn > 0)
        def _():
            chunk_copy(b2, c0n, 0).start()

    acc = o_ref[0]
    rr = acc * lax.rsqrt(jnp.mean(acc * acc, axis=-1, keepdims=True) + NORM_EPS) * gpost_ref[...]
    o_ref[0] = x1_ref[0] + mod_ref[0, 5:6, :] * rr


def _combine(off_flat, ytm, tok_tm, x1, mod, g_post2):
    nb, n, d = x1.shape
    nt = n // STEP
    rows_pb = ytm.shape[0] // nb
    mod_map = (lambda b, i, *_: (b, 0, 0)) if mod.shape[0] == nb else (lambda b, i, *_: (0, 0, 0))
    grid_spec = pltpu.PrefetchScalarGridSpec(
        num_scalar_prefetch=1,
        grid=(nb, nt),
        in_specs=[pl.BlockSpec(memory_space=pl.ANY),
                  pl.BlockSpec((1, rows_pb // STEP, STEP), lambda b, i, *_: (b, 0, 0)),
                  pl.BlockSpec((1, STEP, d), lambda b, i, *_: (b, i, 0)),
                  pl.BlockSpec((1, 6, d), mod_map),
                  pl.BlockSpec((1, d), lambda b, i, *_: (0, 0))],
        out_specs=pl.BlockSpec((1, STEP, d), lambda b, i, *_: (b, i, 0)),
        scratch_shapes=[pltpu.VMEM((2, STEP, d), F32), pltpu.SemaphoreType.DMA((2,))],
    )
    return pl.pallas_call(
        functools.partial(_combine_body, rows_pb=rows_pb, nt=nt, nb=nb),
        grid_spec=grid_spec,
        out_shape=jax.ShapeDtypeStruct((nb, n, d), F32),
        compiler_params=_cparams(("arbitrary", "arbitrary"), 40 * 1024 * 1024),
        name="moe_combine",
    )(off_flat, ytm, tok_tm, x1, mod, g_post2.reshape(1, d))


def _route(aff_t, cap):
    b, n_exp, n = aff_t.shape
    nt = n // STEP
    gate, idx = lax.top_k(aff_t, cap)
    order = jnp.argsort(idx, axis=-1)
    idx = jnp.take_along_axis(idx, order, axis=-1).astype(I32)
    gate = jnp.take_along_axis(gate, order, axis=-1)
    sel = jnp.zeros((b, n_exp, n), I32).at[
        jnp.arange(b)[:, None, None], jnp.arange(n_exp)[None, :, None], idx].set(1)
    rank = jnp.cumsum(sel, axis=1) - sel
    cnt = jnp.sum(sel, axis=1)
    first = jnp.cumsum(cnt, axis=-1) - cnt
    pos = (jnp.take_along_axis(first[:, None, :] + rank, idx, axis=-1)).astype(I32)
    tok_tm = jnp.zeros((b, n_exp * cap), I32).at[jnp.arange(b)[:, None, None], pos].set(idx)
    off = jnp.concatenate([first[:, ::STEP], jnp.full((b, 1), n_exp * cap, first.dtype)], axis=-1).astype(I32)
    return idx, gate, pos, tok_tm, off


def _moe(aff_t, h2, x1, mod, g_post2, w1, w3, w2):
    b, n, d = x1.shape
    assert n <= 1 << TOK_BITS
    cap = EC_CAPACITY * n // N_EXPERTS
    idx, gate, pos, tok_tm, off = _route(aff_t, cap)
    code = (pos << TOK_BITS) | idx
    ytm = _expert_ffn(code.reshape(-1), h2, gate[..., None], w1, w3, w2)
    rows = b * N_EXPERTS * cap
    return _combine(off.reshape(-1), ytm.reshape(rows, d), tok_tm.reshape(b, N_EXPERTS * cap // STEP, STEP),
                    x1, mod, g_post2)


def _rope_tables(n):
    rows = n // GRID_W
    r = jnp.repeat(jnp.arange(rows, dtype=F32), GRID_W)
    col = jnp.tile(jnp.arange(GRID_W, dtype=F32), rows)
    half = HEAD_DIM // 2
    inv = ROPE_BASE ** (-jnp.arange(0, half, 2, dtype=F32) / half)
    ar, ac = r[:, None] * inv, col[:, None] * inv
    cr, sr, cc, sc = jnp.cos(ar), jnp.sin(ar), jnp.cos(ac), jnp.sin(ac)
    return (jnp.concatenate([cr, cr, cc, cc], axis=-1), jnp.concatenate([-sr, sr, -sc, sc], axis=-1))


def _prep_w_in(w_in):
    d = w_in.shape[0]
    main = w_in[:, :QKV_W + REST_GATE]
    gcols = w_in[:, QKV_W + REST_GATE:].reshape(d, 2, 2, DN_HEADS).transpose(0, 2, 1, 3).reshape(d, 4 * DN_HEADS)
    pad = jnp.zeros((d, IN_PAD_W - QKV_W - REST_GATE - 4 * DN_HEADS), w_in.dtype)
    return jnp.concatenate([main, gcols, pad], axis=1).astype(BF16)


def _lane_row(v8):
    return jnp.concatenate([v8.reshape(-1).astype(F32), jnp.zeros((128 - v8.size,), F32)]).reshape(1, 128)


def kernel(x, c, ctx, c_ctx, g_pre1, g_post1, g_pre2, g_post2, w_mod, b_mod, w_in, attn_sink, gm_ln, gm_ws, gm_bs,
           dn_conv, dn_a_log, dn_dt_bias, dn_norm, w_out, w_router, w_e1, w_e3, w_e2):
    b, n, d = x.shape
    depth = w_in.shape[0]
    rope_cs = _rope_tables(n)
    c8 = jnp.concatenate([c, c_ctx[None, :], jnp.zeros((8 - b - 1, d), F32)], axis=0)
    xc = ctx
    for l in range(depth):
        update_ctx = l < depth - 1
        mod_all = _modulation(c8, w_mod, b_mod[l], l)
        mod_lat = mod_all[:b].reshape(b, 6, d)
        mod_ctx = mod_all[b:b + 1].reshape(1, 6, d)
        w_in_bf = _prep_w_in(w_in[l])
        qkv, rest = _inproj(x, mod_lat, g_pre1[l], w_in_bf, rope_cs)
        qkvc, restc = _inproj(xc, mod_ctx, g_pre1[l], w_in_bf, None)
        ya = _attention(qkv, qkvc, attn_sink[l])
        ws_bf = gm_ws[l].astype(BF16)
        yg = _gmlp(rest, gm_ln[l], ws_bf, gm_bs[l])
        (o_f, o_b), (oc_f, oc_b) = _deltanet(rest, restc, _lane_row(dn_a_log[l]), _lane_row(dn_dt_bias[l]),
                                             dn_conv[l])
        w_out_bf = w_out[l].astype(BF16)
        w_router_t = w_router[l].T
        w1, w3, w2 = w_e1[l].astype(BF16), w_e3[l].astype(BF16), w_e2[l].astype(BF16)
        x1, h2, aff_t = _outproj(ya, yg, o_f, o_b, rest, x, mod_lat, dn_norm[l], w_out_bf, g_post1[l], g_pre2[l],
                                 w_router_t)
        x = _moe(aff_t, h2, x1, mod_lat, g_post2[l], w1, w3, w2)
        if update_ctx:
            yac = _ctx_attention(qkvc, attn_sink[l])
            ygc = _gmlp(restc, gm_ln[l], ws_bf, gm_bs[l])
            xc1, h2c, affc_t = _outproj(yac, ygc, oc_f, oc_b, restc, xc, mod_ctx, dn_norm[l], w_out_bf, g_post1[l],
                                        g_pre2[l], w_router_t)
            xc = _moe(affc_t, h2c, xc1, mod_ctx, g_post2[l], w1, w3, w2)
    return x
```

```python
import functools

import jax
import jax.numpy as jnp
from jax import lax
from jax.experimental import pallas as pl
from jax.experimental.pallas import tpu as pltpu

F32 = jnp.float32
BF16 = jnp.bfloat16
I32 = jnp.int32

HEAD_DIM = 128
GRID_W = 64
ATTN_HEADS = 8
ATTN_KV_HEADS = 2
ATTN_GROUP = ATTN_HEADS // ATTN_KV_HEADS
WINDOW = 128
ATTN_BLOCK = 128
ROPE_BASE = 10000.0
MASK_VALUE = -1e30
GM_HEADS = 4
GM_CHUNK = 128
DN_HEADS = 4
DN_CHUNK = 64
N_EXPERTS = 16
EC_CAPACITY = 2
NORM_EPS = 1e-6

ATTN_W = ATTN_HEADS * HEAD_DIM
KV_W = ATTN_KV_HEADS * HEAD_DIM
QKV_W = ATTN_W + 2 * KV_W
GM_W = GM_HEADS * HEAD_DIM
DN_W = DN_HEADS * HEAD_DIM
MIX_W = ATTN_W + GM_W + DN_W
REST_GU, REST_GV, REST_DQ, REST_DK, REST_DV, REST_DZ, REST_GATE = 0, 512, 1024, 1536, 2048, 2560, 3072
REST_W = 3200
IN_PAD_W = QKV_W + REST_W

V7X_VMEM_LIMIT = 56 * 1024 * 1024
STEP = 128


def _cparams(sem, vmem=None):
    return pltpu.CompilerParams(dimension_semantics=sem, vmem_limit_bytes=vmem)


def _silu(x):
    return x * jax.nn.sigmoid(x)


def _dot(a, b):
    return jnp.dot(a, b, preferred_element_type=F32)


def _dot_nt(a, b):
    return lax.dot_general(a, b, (((1,), (1,)), ((), ())), preferred_element_type=F32)


def _split2(a):
    hi = a.astype(BF16)
    lo = (a - hi.astype(F32)).astype(BF16)
    return hi, lo


def _split3(a):
    hi = a.astype(BF16)
    r = a - hi.astype(F32)
    mid = r.astype(BF16)
    lo = (r - mid.astype(F32)).astype(BF16)
    return hi, mid, lo


def _dot_x3(a, b, nt=False):
    d = _dot_nt if nt else _dot
    ah, al = _split2(a)
    bh, bl = _split2(b)
    return d(ah, bh) + d(ah, bl) + d(al, bh)


def _mod_body(c_ref, w_ref, b_ref, o_ref):
    s = _silu(c_ref[...]).astype(BF16)
    o_ref[...] = _dot(s, w_ref[0].astype(BF16)) + b_ref[...]


def _modulation(c8, w_mod_all, b_mod, layer):
    _, d, n6 = w_mod_all.shape
    tn = 1024
    return pl.pallas_call(
        _mod_body,
        grid=(n6 // tn,),
        in_specs=[pl.BlockSpec((8, d), lambda j: (0, 0)),
                  pl.BlockSpec((1, d, tn), lambda j: (layer, 0, j)),
                  pl.BlockSpec((1, tn), lambda j: (0, j))],
        out_specs=pl.BlockSpec((8, tn), lambda j: (0, j)),
        out_shape=jax.ShapeDtypeStruct((8, n6), F32),
        compiler_params=_cparams(("arbitrary",), 40 * 1024 * 1024),
        name="modulation",
    )(c8, w_mod_all, b_mod.reshape(1, n6))


def _inproj_body(*refs, rope, tm):
    if rope:
        x_ref, mod_ref, g_ref, w_ref, cos_ref, sin_ref, qkv_ref, rest_ref = refs
    else:
        x_ref, mod_ref, g_ref, w_ref, qkv_ref, rest_ref = refs
    x = x_ref[0]
    ms = jnp.mean(x * x, axis=-1, keepdims=True)
    y = x * lax.rsqrt(ms + NORM_EPS) * g_ref[...]
    h = y * (1.0 + mod_ref[0, 1:2, :]) + mod_ref[0, 0:1, :]
    hb = h.astype(BF16)
    if rope:
        cosf = cos_ref[...]
        sinf = sin_ref[...]
        lane = lax.broadcasted_iota(I32, (tm, HEAD_DIM), 1)
        first = (lane & 32) == 0

    def rot(t):
        if not rope:
            return t
        partner = jnp.where(first, pltpu.roll(t, 96, 1), pltpu.roll(t, 32, 1))
        return t * cosf + partner * sinf

    for c0 in (0, 512):
        t = _dot(hb, w_ref[:, c0:c0 + 512])
        for j in range(4):
            th = rot(t[:, j * 128:(j + 1) * 128]) * (HEAD_DIM ** -0.5)
            qkv_ref[0, :, c0 + j * 128:c0 + (j + 1) * 128] = th.astype(BF16)
    t = _dot(hb, w_ref[:, ATTN_W:ATTN_W + 512])
    for j in range(2):
        qkv_ref[0, :, ATTN_W + j * 128:ATTN_W + (j + 1) * 128] = rot(t[:, j * 128:(j + 1) * 128]).astype(BF16)
    qkv_ref[0, :, ATTN_W + KV_W:QKV_W] = t[:, 256:512].astype(BF16)
    for c0 in range(0, REST_W, 512):
        cw = min(512, REST_W - c0)
        rest_ref[0, :, c0:c0 + cw] = _dot(hb, w_ref[:, QKV_W + c0:QKV_W + c0 + cw])


def _inproj(x, mod, g, w_bf, rope_cs):
    b, t, d = x.shape
    tm = 256
    rope = rope_cs is not None
    mod_map = (lambda bi, i: (bi, 0, 0)) if mod.shape[0] == b else (lambda bi, i: (0, 0, 0))
    in_specs = [pl.BlockSpec((1, tm, d), lambda bi, i: (bi, i, 0)),
                pl.BlockSpec((1, 6, d), mod_map),
                pl.BlockSpec((1, d), lambda bi, i: (0, 0)),
                pl.BlockSpec((d, IN_PAD_W), lambda bi, i: (0, 0), pipeline_mode=pl.Buffered(1))]
    args = [x, mod, g.reshape(1, d), w_bf]
    if rope:
        in_specs += [pl.BlockSpec((tm, HEAD_DIM), lambda bi, i: (i, 0)),
                     pl.BlockSpec((tm, HEAD_DIM), lambda bi, i: (i, 0))]
        args += list(rope_cs)
    return pl.pallas_call(
        functools.partial(_inproj_body, rope=rope, tm=tm),
        grid=(b, t // tm),
        in_specs=in_specs,
        out_specs=[pl.BlockSpec((1, tm, QKV_W), lambda bi, i: (bi, i, 0)),
                   pl.BlockSpec((1, tm, REST_W), lambda bi, i: (bi, i, 0))],
        out_shape=[jax.ShapeDtypeStruct((b, t, QKV_W), BF16),
                   jax.ShapeDtypeStruct((b, t, REST_W), F32)],
        compiler_params=_cparams(("parallel", "arbitrary"), V7X_VMEM_LIMIT),
        name="inproj_rope" if rope else "inproj",
    )(*args)


def _group_queries(q_ref, rows, kh):
    return jnp.concatenate(
        [q_ref[0, rows, (kh * ATTN_GROUP + g) * HEAD_DIM:(kh * ATTN_GROUP + g + 1) * HEAD_DIM]
         for g in range(ATTN_GROUP)], axis=0)


def _softmax_pv_blocks(sink_ref, o_ref, blocks, scores, values):
    probs, dens = {}, {}
    for blk in blocks:
        _, kh = blk
        sink_col = jnp.concatenate(
            [jnp.full((ATTN_BLOCK, 1), sink_ref[kh * ATTN_GROUP + g], F32) for g in range(ATTN_GROUP)], axis=0)
        mx = sink_col
        for s in scores[blk]:
            mx = jnp.maximum(mx, jnp.max(s, axis=-1, keepdims=True))
        probs[blk] = [jnp.exp(s - mx).astype(BF16) for s in scores[blk]]
        dens[blk] = jnp.exp(sink_col - mx)
    for blk in blocks:
        j, kh = blk
        acc = None
        for e, v in zip(probs[blk], values[blk]):
            pv = _dot(e, jnp.concatenate([v, jnp.ones_like(v)], axis=1))
            acc = pv if acc is None else acc + pv
        o = acc[:, :HEAD_DIM] / (dens[blk] + acc[:, HEAD_DIM:HEAD_DIM + 1])
        rows = slice(j * ATTN_BLOCK, (j + 1) * ATTN_BLOCK)
        for g in range(ATTN_GROUP):
            hh = kh * ATTN_GROUP + g
            o_ref[0, rows, hh * HEAD_DIM:(hh + 1) * HEAD_DIM] = o[g * ATTN_BLOCK:(g + 1) * ATTN_BLOCK].astype(BF16)


def _attn_body(sink_ref, q_ref, km_ref, vm_ref, kp_ref, vp_ref, kn_ref, vn_ref, kc_ref, vc_ref, o_ref, *, tq, n):
    i = pl.program_id(1)
    qb = tq // ATTN_BLOCK
    qi = lax.broadcasted_iota(I32, (ATTN_BLOCK, 3 * ATTN_BLOCK), 0)
    sj = lax.broadcasted_iota(I32, (ATTN_BLOCK, 3 * ATTN_BLOCK), 1) - ATTN_BLOCK

    def band(main_ref, prev_ref, next_ref, j, hs):
        pieces = []
        for blk in (j - 1, j, j + 1):
            if blk < 0:
                pieces.append(prev_ref[0, :, hs])
            elif blk >= qb:
                pieces.append(next_ref[0, :, hs])
            else:
                pieces.append(main_ref[0, blk * ATTN_BLOCK:(blk + 1) * ATTN_BLOCK, hs])
        return jnp.concatenate(pieces, axis=0)

    blocks = [(j, kh) for j in range(qb) for kh in range(ATTN_KV_HEADS)]
    scores, values = {}, {}
    for j in range(qb):
        kpos = (i * qb + j) * ATTN_BLOCK + sj
        ok = (jnp.abs(sj - qi) <= WINDOW) & (kpos >= 0) & (kpos < n)
        bias = jnp.where(ok, 0.0, MASK_VALUE).astype(F32)
        bias4 = jnp.concatenate([bias] * ATTN_GROUP, axis=0)
        rows = slice(j * ATTN_BLOCK, (j + 1) * ATTN_BLOCK)
        for kh in range(ATTN_KV_HEADS):
            hs = slice(kh * HEAD_DIM, (kh + 1) * HEAD_DIM)
            q = _group_queries(q_ref, rows, kh)
            scores[(j, kh)] = [_dot_nt(q, band(km_ref, kp_ref, kn_ref, j, hs)) + bias4,
                               _dot_nt(q, kc_ref[0, :, hs])]
            values[(j, kh)] = [band(vm_ref, vp_ref, vn_ref, j, hs), vc_ref[0, :, hs]]
    _softmax_pv_blocks(sink_ref, o_ref, blocks, scores, values)


def _attention(qkv, qkvc, sink):
    b, n, _ = qkv.shape
    m = qkvc.shape[1]
    tq = 512
    qb = tq // ATTN_BLOCK
    nb = n // ATTN_BLOCK
    kcol, vcol = ATTN_W // KV_W, ATTN_W // KV_W + 1
    smem = pl.BlockSpec(memory_space=pltpu.SMEM)
    main = lambda col: pl.BlockSpec((1, tq, KV_W), lambda bi, i: (bi, i, col))
    prev = lambda col: pl.BlockSpec((1, ATTN_BLOCK, KV_W), lambda bi, i: (bi, jnp.maximum(i * qb - 1, 0), col))
    nxt = lambda col: pl.BlockSpec((1, ATTN_BLOCK, KV_W), lambda bi, i: (bi, jnp.minimum(i * qb + qb, nb - 1), col))
    ctx = lambda col: pl.BlockSpec((1, m, KV_W), lambda bi, i: (bi, 0, col))
    return pl.pallas_call(
        functools.partial(_attn_body, tq=tq, n=n),
        grid=(b, n // tq),
        in_specs=[smem, pl.BlockSpec((1, tq, ATTN_W), lambda bi, i: (bi, i, 0)),
                  main(kcol), main(vcol), prev(kcol), prev(vcol), nxt(kcol), nxt(vcol), ctx(kcol), ctx(vcol)],
        out_specs=pl.BlockSpec((1, tq, ATTN_W), lambda bi, i: (bi, i, 0)),
        out_shape=jax.ShapeDtypeStruct((b, n, ATTN_W), BF16),
        compiler_params=_cparams(("parallel", "arbitrary"), 40 * 1024 * 1024),
        name="window_attention",
    )(sink, qkv, qkv, qkv, qkv, qkv, qkv, qkv, qkvc, qkvc)


def _ctx_attn_body(sink_ref, q_ref, kc_ref, vc_ref, o_ref, *, m):
    blocks = [(j, kh) for j in range(m // ATTN_BLOCK) for kh in range(ATTN_KV_HEADS)]
    scores, values = {}, {}
    for j, kh in blocks:
        rows = slice(j * ATTN_BLOCK, (j + 1) * ATTN_BLOCK)
        hs = slice(kh * HEAD_DIM, (kh + 1) * HEAD_DIM)
        scores[(j, kh)] = [_dot_nt(_group_queries(q_ref, rows, kh), kc_ref[0, :, hs])]
        values[(j, kh)] = [vc_ref[0, :, hs]]
    _softmax_pv_blocks(sink_ref, o_ref, blocks, scores, values)


def _ctx_attention(qkvc, sink):
    b, m, _ = qkvc.shape
    kcol, vcol = ATTN_W // KV_W, ATTN_W // KV_W + 1
    return pl.pallas_call(
        functools.partial(_ctx_attn_body, m=m),
        grid=(b,),
        in_specs=[pl.BlockSpec(memory_space=pltpu.SMEM),
                  pl.BlockSpec((1, m, ATTN_W), lambda bi: (bi, 0, 0)),
                  pl.BlockSpec((1, m, KV_W), lambda bi: (bi, 0, kcol)),
                  pl.BlockSpec((1, m, KV_W), lambda bi: (bi, 0, vcol))],
        out_specs=pl.BlockSpec((1, m, ATTN_W), lambda bi: (bi, 0, 0)),
        out_shape=jax.ShapeDtypeStruct((b, m, ATTN_W), BF16),
        compiler_params=_cparams(("parallel",), 40 * 1024 * 1024),
        name="context_attention",
    )(sink, qkvc, qkvc, qkvc)


def _gmlp_body(u_ref, v_ref, ln_ref, ws_ref, bst_ref, o_ref, *, tg):
    for ci in range(tg // GM_CHUNK):
        rows = slice(ci * GM_CHUNK, (ci + 1) * GM_CHUNK)
        u = jax.nn.gelu(u_ref[0, rows, :])
        v = jax.nn.gelu(v_ref[0, rows, :])
        for h in range(GM_HEADS):
            hs = slice(h * HEAD_DIM, (h + 1) * HEAD_DIM)
            vh = v[:, hs]
            vh = vh - jnp.mean(vh, axis=-1, keepdims=True)
            vh = vh * lax.rsqrt(jnp.mean(vh * vh, axis=-1, keepdims=True) + NORM_EPS) * ln_ref[:, hs]
            mixed = _dot(ws_ref[h], vh.astype(BF16)) + bst_ref[:, h:h + 1]
            o_ref[0, rows, hs] = (u[:, hs] * mixed).astype(BF16)


def _gmlp(rest, gm_ln, gm_ws_bf, gm_bs):
    b, n, _ = rest.shape
    tg = min(512, n)
    return pl.pallas_call(
        functools.partial(_gmlp_body, tg=tg),
        grid=(b, n // tg),
        in_specs=[pl.BlockSpec((1, tg, GM_W), lambda bi, i: (bi, i, REST_GU // GM_W)),
                  pl.BlockSpec((1, tg, GM_W), lambda bi, i: (bi, i, REST_GV // GM_W)),
                  pl.BlockSpec((1, GM_W), lambda bi, i: (0, 0)),
                  pl.BlockSpec((GM_HEADS, GM_CHUNK, GM_CHUNK), lambda bi, i: (0, 0, 0)),
                  pl.BlockSpec((GM_CHUNK, GM_HEADS), lambda bi, i: (0, 0))],
        out_specs=pl.BlockSpec((1, tg, GM_W), lambda bi, i: (bi, i, 0)),
        out_shape=jax.ShapeDtypeStruct((b, n, GM_W), BF16),
        compiler_params=_cparams(("parallel", "arbitrary"), 40 * 1024 * 1024),
        name="gmlp",
    )(rest, rest, gm_ln.reshape(1, GM_W), gm_ws_bf, gm_bs.T)


def _dn_pre_body(alog_ref, dtb_ref, cw_ref, q_ref, k_ref, v_ref, qp_ref, kp_ref, vp_ref,
                 qn_ref, kn_ref, vn_ref, gate_ref, oq_ref, ok_ref, ov_ref, g_ref, gt_ref, *, td, nt):
    i = pl.program_id(1)
    row = lax.broadcasted_iota(I32, (td, DN_W), 0)
    has_prev = (i > 0).astype(F32)
    has_next = (i < nt - 1).astype(F32)

    def conv(x_ref, p_ref, n_ref, c0):
        x = x_ref[0]
        prev_row = p_ref[0, 7:8, :] * has_prev
        next_row = n_ref[0, 0:1, :] * has_next
        xm = jnp.where(row == 0, prev_row, pltpu.roll(x, 1, 0))
        xp = jnp.where(row == td - 1, next_row, pltpu.roll(x, td - 1, 0))
        y = (cw_ref[0:1, c0:c0 + DN_W] * xm + cw_ref[1:2, c0:c0 + DN_W] * x
             + cw_ref[2:3, c0:c0 + DN_W] * xp)
        return _silu(y)

    def l2n(y, h):
        yh = y[:, h * HEAD_DIM:(h + 1) * HEAD_DIM]
        return yh * lax.rsqrt(jnp.sum(yh * yh, axis=-1, keepdims=True) + NORM_EPS)

    yq = conv(q_ref, qp_ref, qn_ref, 0)
    yk = conv(k_ref, kp_ref, kn_ref, DN_W)
    for h in range(DN_HEADS):
        hs = slice(h * HEAD_DIM, (h + 1) * HEAD_DIM)
        oq_ref[0, :, hs] = l2n(yq, h) * (HEAD_DIM ** -0.5)
        ok_ref[0, :, hs] = l2n(yk, h)
    ov_ref[0] = conv(v_ref, vp_ref, vn_ref, 2 * DN_W)

    raw = gate_ref[0]
    z = raw + dtb_ref[...]
    softplus = jnp.maximum(z, 0.0) + jnp.log(1.0 + jnp.exp(-jnp.abs(z)))
    gval = -jnp.exp(alog_ref[...]) * softplus
    lane = lax.broadcasted_iota(I32, (td, 128), 1)
    gates = jnp.where(lane < 2 * DN_HEADS, gval, jax.nn.sigmoid(raw))
    g_ref[0] = gates
    gt_ref[0] = gates.T[0:16, :]


def _dn_pre(rest, alog_row, dtb_row, conv_w):
    b, n, _ = rest.shape
    td = 256
    nt = n // td
    cur = lambda col: pl.BlockSpec((1, td, DN_W), lambda bi, i: (bi, i, col))
    prev = lambda col: pl.BlockSpec((1, 8, DN_W), lambda bi, i: (bi, jnp.maximum(i * (td // 8) - 1, 0), col))
    nxt = lambda col: pl.BlockSpec((1, 8, DN_W), lambda bi, i: (bi, jnp.minimum((i + 1) * (td // 8), n // 8 - 1), col))
    cq, ck, cv = REST_DQ // DN_W, REST_DK // DN_W, REST_DV // DN_W
    row128 = pl.BlockSpec((1, 128), lambda bi, i: (0, 0))
    tok = lambda w: pl.BlockSpec((1, td, w), lambda bi, i: (bi, i, 0))
    return pl.pallas_call(
        functools.partial(_dn_pre_body, td=td, nt=nt),
        grid=(b, nt),
        in_specs=[row128, row128, pl.BlockSpec((3, 3 * DN_W), lambda bi, i: (0, 0)),
                  cur(cq), cur(ck), cur(cv), prev(cq), prev(ck), prev(cv), nxt(cq), nxt(ck), nxt(cv),
                  pl.BlockSpec((1, td, 128), lambda bi, i: (bi, i, REST_GATE // 128))],
        out_specs=[tok(DN_W), tok(DN_W), tok(DN_W), tok(128),
                   pl.BlockSpec((1, 16, td), lambda bi, i: (bi, 0, i))],
        out_shape=[jax.ShapeDtypeStruct((b, n, DN_W), F32)] * 3
        + [jax.ShapeDtypeStruct((b, n, 128), F32), jax.ShapeDtypeStruct((b, 16, n), F32)],
        compiler_params=_cparams(("parallel", "arbitrary"), 40 * 1024 * 1024),
        name="deltanet_pre",
    )(alog_row, dtb_row, conv_w, rest, rest, rest, rest, rest, rest, rest, rest, rest, rest)


def _dn_chunk_body(q_ref, k_ref, v_ref, g_ref, gt_ref, *out_refs, nb):
    dir_refs = (out_refs[:6], out_refs[6:])
    r = lax.broadcasted_iota(I32, (STEP, STEP), 0)
    c = lax.broadcasted_iota(I32, (STEP, STEP), 1)
    same = (r // DN_CHUNK) == (c // DN_CHUNK)
    incl = (same & (c <= r), same & (c >= r))
    strict = (same & (c < r), same & (c > r))
    lower_b = jnp.where(incl[0], 1.0, 0.0).astype(BF16)
    upper_b = jnp.where(incl[1], 1.0, 0.0).astype(BF16)
    same_b = jnp.where(same, 1.0, 0.0).astype(BF16)
    lane = lax.broadcasted_iota(I32, (STEP, 128), 1)

    gates, gc_col, gc_row, tot_col, tot_row = [], [], [], [], []
    for bi in range(nb):
        g = g_ref[bi]
        gt = gt_ref[bi]
        gh, gm, gl = _split3(g)
        th, tm_, tl = _split3(gt)
        cf_col = _dot(lower_b, gh) + _dot(lower_b, gm) + _dot(lower_b, gl)
        tc = _dot(same_b, gh) + _dot(same_b, gm) + _dot(same_b, gl)
        cf_row = _dot(th, upper_b) + _dot(tm_, upper_b) + _dot(tl, upper_b)
        tr = _dot(th, same_b) + _dot(tm_, same_b) + _dot(tl, same_b)
        gates.append(g)
        gc_col.append((cf_col, tc - cf_col + g))
        gc_row.append((cf_row, tr - cf_row + gt))
        tot_col.append(tc)
        tot_row.append(tr)

    heads = [(bi, h) for bi in range(nb) for h in range(DN_HEADS)]
    hsl = lambda h: slice(h * HEAD_DIM, (h + 1) * HEAD_DIM)
    kbf = {bh: k_ref[bh[0], :, hsl(bh[1])].astype(BF16) for bh in heads}
    kk = {bh: _dot_nt(kbf[bh], kbf[bh]) for bh in heads}
    qk = {bh: _dot_nt(q_ref[bh[0], :, hsl(bh[1])].astype(BF16), kbf[bh]) for bh in heads}

    probs = [(bi, h, d) for bi, h in heads for d in range(2)]
    dec, y, p = {}, {}, {}
    for pr in probs:
        bi, h, d = pr
        lg = d * DN_HEADS + h
        lb = 2 * DN_HEADS + d * DN_HEADS + h
        gcc = gc_col[bi][d][:, lg:lg + 1]
        gcr = gc_row[bi][d][lg:lg + 1, :]
        dec[pr] = jnp.exp(jnp.where(incl[d], gcc - gcr, MASK_VALUE))
        y[pr] = -jnp.where(strict[d], kk[(bi, h)] * gates[bi][:, lb:lb + 1] * dec[pr], 0.0)
        p[pr] = y[pr]
    for _ in range(5):
        for pr in probs:
            pb = p[pr].astype(BF16)
            p[pr] = _dot(pb, pb)
        for pr in probs:
            y[pr] = y[pr] + p[pr] + _dot(y[pr].astype(BF16), p[pr].astype(BF16))

    for pr in probs:
        bi, h, d = pr
        hs = hsl(h)
        u_ref, w_ref, qg_ref, kgt_ref, intra_ref, _ = dir_refs[d]
        lg = d * DN_HEADS + h
        lb = 2 * DN_HEADS + d * DN_HEADS + h
        gcc = gc_col[bi][d][:, lg:lg + 1]
        gcr = gc_row[bi][d][lg:lg + 1, :]
        totr = tot_row[bi][lg:lg + 1, :]
        beta = gates[bi][:, lb:lb + 1]
        q = q_ref[bi, :, hs]
        k = k_ref[bi, :, hs]
        egc = jnp.exp(gcc)
        rhs = jnp.concatenate([v_ref[bi, :, hs] * beta, k * (beta * egc)], axis=1)
        sol = rhs + _dot(y[pr].astype(BF16), rhs.astype(BF16))
        u_ref[bi, :, hs] = sol[:, :HEAD_DIM]
        w_ref[bi, :, hs] = sol[:, HEAD_DIM:].astype(BF16)
        intra_ref[bi, :, hs] = (qk[(bi, h)] * dec[pr]).astype(BF16)
        qg_ref[bi, :, hs] = (q * egc).astype(BF16)
        kgt_ref[bi, hs, :] = (k.T * jnp.exp(totr - gcr)).astype(BF16)

    for bi in range(nb):
        for d in range(2):
            eg = jnp.zeros((STEP, 128), F32)
            for h in range(DN_HEADS):
                lg = d * DN_HEADS + h
                eg = jnp.where(lane == h, jnp.exp(tot_col[bi][:, lg:lg + 1]), eg)
            dir_refs[d][5][bi] = eg


DN_CHUNK_NB = 2


def _dn_chunk(dq, dk, dv, gates, gates_t):
    b, n, _ = dq.shape
    nb = DN_CHUNK_NB if b % DN_CHUNK_NB == 0 else 1
    tok = lambda w: pl.BlockSpec((nb, STEP, w), lambda bi, i: (bi, i, 0))
    one_dir_specs = [tok(DN_W), tok(DN_W), tok(DN_W),
                     pl.BlockSpec((nb, DN_W, STEP), lambda bi, i: (bi, 0, i)), tok(DN_W), tok(128)]
    one_dir_shapes = [jax.ShapeDtypeStruct((b, n, DN_W), F32), jax.ShapeDtypeStruct((b, n, DN_W), BF16),
                      jax.ShapeDtypeStruct((b, n, DN_W), BF16), jax.ShapeDtypeStruct((b, DN_W, n), BF16),
                      jax.ShapeDtypeStruct((b, n, DN_W), BF16), jax.ShapeDtypeStruct((b, n, 128), F32)]
    outs = pl.pallas_call(
        functools.partial(_dn_chunk_body, nb=nb),
        grid=(b // nb, n // STEP),
        in_specs=[tok(DN_W), tok(DN_W), tok(DN_W), tok(128),
                  pl.BlockSpec((nb, 16, STEP), lambda bi, i: (bi, 0, i))],
        out_specs=one_dir_specs * 2,
        out_shape=one_dir_shapes * 2,
        compiler_params=_cparams(("parallel", "arbitrary"), 48 * 1024 * 1024),
        name="deltanet_chunk",
    )(dq, dk, dv, gates, gates_t)
    return outs[:6], outs[6:]


def _dn_scan_body(*refs, nb, ns):
    dir_in = (refs[0:7], refs[7:14])
    o_refs = refs[14:16]
    sout_refs = refs[16:18]
    s_scr = refs[18]
    i = pl.program_id(0)

    @pl.when(i == 0)
    def _():
        s_scr[0] = dir_in[0][6][...]
        s_scr[1] = dir_in[1][6][...]

    zeros = jnp.zeros((DN_CHUNK, HEAD_DIM), F32)
    chains = [(d, bi, h) for d in range(2) for bi in range(nb) for h in range(DN_HEADS)]
    state = {ch: s_scr[ch[0], ch[1], ch[2]] for ch in chains}
    for pos in range(2):
        ws = {}
        for ch in chains:
            d, bi, h = ch
            ci = pos if d == 0 else 1 - pos
            rows = slice(ci * DN_CHUNK, (ci + 1) * DN_CHUNK)
            hs = slice(h * HEAD_DIM, (h + 1) * HEAD_DIM)
            wq = jnp.concatenate([dir_in[d][1][bi, rows, hs], dir_in[d][2][bi, rows, hs]], axis=0)
            ws[ch] = _dot(wq, state[ch].astype(BF16))
        for ch in chains:
            d, bi, h = ch
            u_ref, _, _, kgt_ref, intra_ref, eg_ref, _ = dir_in[d]
            ci = pos if d == 0 else 1 - pos
            rows = slice(ci * DN_CHUNK, (ci + 1) * DN_CHUNK)
            hs = slice(h * HEAD_DIM, (h + 1) * HEAD_DIM)
            v_new = u_ref[bi, rows, hs] - ws[ch][:DN_CHUNK]
            pieces = [v_new, zeros] if ci == 0 else [zeros, v_new]
            v_full = jnp.concatenate(pieces, axis=0).astype(BF16)
            o_refs[d][bi, rows, hs] = ws[ch][DN_CHUNK:] + _dot(intra_ref[bi, rows, hs], v_full)
            e = eg_ref[bi, rows, h:h + 1]
            state[ch] = state[ch] * jnp.concatenate([e, e], axis=0) + _dot(kgt_ref[bi, hs, :], v_full)
    for ch in chains:
        s_scr[ch[0], ch[1], ch[2]] = state[ch]

    @pl.when(i == ns - 1)
    def _():
        sout_refs[0][...] = s_scr[0]
        sout_refs[1][...] = s_scr[1]


def _dn_scan(fwd, bwd, s0_f, s0_b):
    b, n, _ = fwd[0].shape
    ns = n // STEP

    def specs(pos):
        tok = lambda wd: pl.BlockSpec((b, STEP, wd), lambda i: (0, pos(i), 0))
        return [tok(DN_W), tok(DN_W), tok(DN_W), pl.BlockSpec((b, DN_W, STEP), lambda i: (0, 0, pos(i))),
                tok(DN_W), tok(128), state]

    state = pl.BlockSpec((b, DN_HEADS, HEAD_DIM, HEAD_DIM), lambda i: (0, 0, 0, 0))
    fpos = lambda i: i
    bpos = lambda i: ns - 1 - i
    state_shape = jax.ShapeDtypeStruct((b, DN_HEADS, HEAD_DIM, HEAD_DIM), F32)
    return pl.pallas_call(
        functools.partial(_dn_scan_body, nb=b, ns=ns),
        grid=(ns,),
        in_specs=specs(fpos) + specs(bpos),
        out_specs=[pl.BlockSpec((b, STEP, DN_W), lambda i: (0, fpos(i), 0)),
                   pl.BlockSpec((b, STEP, DN_W), lambda i: (0, bpos(i), 0)), state, state],
        out_shape=[jax.ShapeDtypeStruct((b, n, DN_W), F32), jax.ShapeDtypeStruct((b, n, DN_W), F32),
                   state_shape, state_shape],
        scratch_shapes=[pltpu.VMEM((2, b, DN_HEADS, HEAD_DIM, HEAD_DIM), F32)],
        compiler_params=_cparams(("arbitrary",), 48 * 1024 * 1024),
        name="deltanet_scan",
    )(*fwd, s0_f, *bwd, s0_b)


def _deltanet(rest, restc, alog_row, dtb_row, conv_w):
    b = rest.shape[0]
    zero_state = jnp.zeros((b, DN_HEADS, HEAD_DIM, HEAD_DIM), F32)
    cf, cb = _dn_chunk(*_dn_pre(restc, alog_row, dtb_row, conv_w))
    oc_f, oc_b, sc_f, sc_b = _dn_scan(cf, cb, zero_state, zero_state)
    lf, lb = _dn_chunk(*_dn_pre(rest, alog_row, dtb_row, conv_w))
    ol_f, ol_b, _, _ = _dn_scan(lf, lb, sc_f, sc_b)
    return (ol_f, ol_b), (oc_f, oc_b)


OUTPROJ_SUB = 128


def _outproj_body(ya_ref, yg_ref, of_ref, ob_ref, z_ref, x_ref, mod_ref, dnn_ref, w_ref, gpost_ref, gpre2_ref,
                  wr_ref, x1_ref, h2_ref, aff_ref, *, tm):
    def project(rows):
        o = of_ref[0, rows, :] + ob_ref[0, rows, :]
        z = z_ref[0, rows, :]
        acc = _dot(ya_ref[0, rows, :], w_ref[0:ATTN_W, :]) + _dot(yg_ref[0, rows, :], w_ref[ATTN_W:ATTN_W + GM_W, :])
        for h in range(DN_HEADS):
            hs = slice(h * HEAD_DIM, (h + 1) * HEAD_DIM)
            oh = o[:, hs]
            yh = oh * lax.rsqrt(jnp.mean(oh * oh, axis=-1, keepdims=True) + NORM_EPS) * dnn_ref[...]
            yd = (yh * _silu(z[:, hs])).astype(BF16)
            r0 = ATTN_W + GM_W + h * HEAD_DIM
            acc = acc + _dot(yd, w_ref[r0:r0 + HEAD_DIM, :])
        return acc

    def finish(rows, acc):
        r = acc * lax.rsqrt(jnp.mean(acc * acc, axis=-1, keepdims=True) + NORM_EPS) * gpost_ref[...]
        x1 = x_ref[0, rows, :] + mod_ref[0, 2:3, :] * r
        x1_ref[0, rows, :] = x1
        y2 = x1 * lax.rsqrt(jnp.mean(x1 * x1, axis=-1, keepdims=True) + NORM_EPS) * gpre2_ref[...]
        h2 = y2 * (1.0 + mod_ref[0, 4:5, :]) + mod_ref[0, 3:4, :]
        h2_ref[0, rows, :] = h2
        logits_t = _dot_x3(wr_ref[...], h2, nt=True)
        e = jnp.exp(logits_t - jnp.max(logits_t, axis=0, keepdims=True))
        aff_ref[0, :, rows] = e / jnp.sum(e, axis=0, keepdims=True)

    sub = min(OUTPROJ_SUB, tm)
    blocks = [slice(r0, r0 + sub) for r0 in range(0, tm, sub)]
    pending = None
    for rows in blocks:
        acc = project(rows)
        if pending is not None:
            finish(*pending)
        pending = (rows, acc)
    finish(*pending)


def _outproj(ya, yg, o_f, o_b, rest, x, mod, dn_norm, w_out_bf, g_post1, g_pre2, w_router_t):
    b, n, d = x.shape
    tm = min(512, n)
    mod_map = (lambda bi, i: (bi, 0, 0)) if mod.shape[0] == b else (lambda bi, i: (0, 0, 0))
    tok = lambda w: pl.BlockSpec((1, tm, w), lambda bi, i: (bi, i, 0))
    vec = lambda w: pl.BlockSpec((1, w), lambda bi, i: (0, 0))
    return pl.pallas_call(
        functools.partial(_outproj_body, tm=tm),
        grid=(b, n // tm),
        in_specs=[tok(ATTN_W), tok(GM_W), tok(DN_W), tok(DN_W),
                  pl.BlockSpec((1, tm, DN_W), lambda bi, i: (bi, i, REST_DZ // DN_W)),
                  tok(d), pl.BlockSpec((1, 6, d), mod_map), vec(HEAD_DIM),
                  pl.BlockSpec((MIX_W, d), lambda bi, i: (0, 0), pipeline_mode=pl.Buffered(1)),
                  vec(d), vec(d), pl.BlockSpec((N_EXPERTS, d), lambda bi, i: (0, 0))],
        out_specs=[tok(d), tok(d), pl.BlockSpec((1, N_EXPERTS, tm), lambda bi, i: (bi, 0, i))],
        out_shape=[jax.ShapeDtypeStruct((b, n, d), F32), jax.ShapeDtypeStruct((b, n, d), F32),
                   jax.ShapeDtypeStruct((b, N_EXPERTS, n), F32)],
        compiler_params=_cparams(("parallel", "arbitrary"), V7X_VMEM_LIMIT),
        name="outproj_router",
    )(ya, yg, o_f, o_b, rest, x, mod, dn_norm.reshape(1, HEAD_DIM), w_out_bf, g_post1.reshape(1, d),
      g_pre2.reshape(1, d), w_router_t)


SUBLANES = 8
TOK_BITS = 12


def _ffn_body(code_ref, h2_ref, gate_ref, w1_ref, w3_ref, w2_ref, ytm_ref, xs_scr, ys_scr, gsem, ssem,
              *, cap, n_exp, nb):
    e = pl.program_id(0)
    b = pl.program_id(1)
    t = e * nb + b
    slot = t & 1
    groups = cap // SUBLANES
    last = n_exp * nb - 1

    def gather(e2, b2, slot2):
        base = (b2 * n_exp + e2) * cap

        def body(g, carry):
            for j in range(SUBLANES):
                tok = code_ref[base + g * SUBLANES + j] & ((1 << TOK_BITS) - 1)
                pltpu.make_async_copy(h2_ref.at[b2, tok >> 3, pl.ds(tok & 7, 1)],
                                      xs_scr.at[slot2, g, pl.ds(j, 1)], gsem.at[slot2]).start()
            return carry
        lax.fori_loop(0, groups, body, 0)

    def scatter_wait(slot2):
        pltpu.make_async_copy(ys_scr.at[slot2], ytm_ref.at[pl.ds(0, groups)], ssem.at[slot2]).wait()

    @pl.when(t == 0)
    def _():
        gather(0, 0, 0)

    @pl.when(t < last)
    def _():
        wrap = b + 1 == nb
        gather(jnp.where(wrap, e + 1, e), jnp.where(wrap, 0, b + 1), 1 - slot)

    pltpu.make_async_copy(h2_ref.at[0, pl.ds(0, groups)], xs_scr.at[slot], gsem.at[slot]).wait()

    @pl.when(t >= 2)
    def _():
        scatter_wait(slot)

    gpr = min(groups, 256 // SUBLANES)
    for g0 in range(0, groups, gpr):
        rows = gpr * SUBLANES
        xb = xs_scr[slot, g0:g0 + gpr].reshape(rows, -1).astype(BF16)
        act = (_silu(_dot(xb, w1_ref[0])) * _dot(xb, w3_ref[0])).astype(BF16)
        y = _dot(act, w2_ref[0]) * gate_ref[0, 0, g0 * SUBLANES:g0 * SUBLANES + rows, :]
        ys_scr[slot, g0:g0 + gpr] = y.reshape(gpr, SUBLANES, -1)

    base = (b * n_exp + e) * cap
    row0 = b * n_exp * cap

    def scatter_body(g, carry):
        for j in range(SUBLANES):
            p = row0 + (code_ref[base + g * SUBLANES + j] >> TOK_BITS)
            pltpu.make_async_copy(ys_scr.at[slot, g, pl.ds(j, 1)], ytm_ref.at[p >> 3, pl.ds(p & 7, 1)],
                                  ssem.at[slot]).start()
        return carry
    lax.fori_loop(0, groups, scatter_body, 0)

    @pl.when(t == last)
    def _():
        if last >= 1:
            scatter_wait(1 - slot)
        scatter_wait(slot)


def _expert_ffn(code_flat, h2, gate, w1, w3, w2, layer):
    nb, n, d = h2.shape
    _, n_exp, _, ff = w1.shape
    cap = code_flat.shape[0] // (n_exp * nb)
    groups = cap // SUBLANES
    grid_spec = pltpu.PrefetchScalarGridSpec(
        num_scalar_prefetch=1,
        grid=(n_exp, nb),
        in_specs=[pl.BlockSpec(memory_space=pl.ANY),
                  pl.BlockSpec((1, 1, cap, 1), lambda e, b, code: (b, e, 0, 0)),
                  pl.BlockSpec((None, 1, d, ff), lambda e, b, code: (layer, e, 0, 0)),
                  pl.BlockSpec((None, 1, d, ff), lambda e, b, code: (layer, e, 0, 0)),
                  pl.BlockSpec((None, 1, ff, d), lambda e, b, code: (layer, e, 0, 0))],
        out_specs=pl.BlockSpec(memory_space=pl.ANY),
        scratch_shapes=[pltpu.VMEM((2, groups, SUBLANES, d), F32), pltpu.VMEM((2, groups, SUBLANES, d), F32),
                        pltpu.SemaphoreType.DMA((2,)), pltpu.SemaphoreType.DMA((2,))],
    )
    return pl.pallas_call(
        functools.partial(_ffn_body, cap=cap, n_exp=n_exp, nb=nb),
        grid_spec=grid_spec,
        out_shape=jax.ShapeDtypeStruct((nb * n_exp * groups, SUBLANES, d), F32),
        compiler_params=_cparams(("arbitrary", "arbitrary"), V7X_VMEM_LIMIT),
        name="expert_ffn",
    )(code_flat, h2.reshape(nb, n // SUBLANES, SUBLANES, d), gate, w1, w3, w2)


COMBINE_SLOTS = 4


def _combine_body(off_ref, ytm_ref, tok_ref, x1_ref, mod_ref, gpost_ref, o_ref, ybuf, sem, prog,
                  *, rows_pb, nt, nb):
    b = pl.program_id(0)
    tile = pl.program_id(1)
    d = o_ref.shape[-1]
    total = rows_pb // STEP

    def chunk_copy(c):
        slot = c & (COMBINE_SLOTS - 1)
        return pltpu.make_async_copy(ytm_ref.at[pl.ds(b * rows_pb + c * STEP, STEP)], ybuf.at[slot], sem.at[slot])

    @pl.when(tile == 0)
    def _():
        prog[0] = 0
        prog[1] = 0

    o0 = off_ref[b * (nt + 1) + tile]
    o1 = off_ref[b * (nt + 1) + tile + 1]
    c0 = o0 >> 7
    nchunks = jnp.where(o1 > o0, ((o1 + STEP - 1) >> 7) - c0, 0)

    o_ref[0] = jnp.zeros((STEP, d), F32)
    first = tok_ref[0, :, 0:1]
    end = first + tok_ref[0, :, 1:2]
    lane = lax.broadcasted_iota(I32, (STEP, STEP), 1)

    def body(i, carry):
        c = c0 + i
        issued = prog[0]
        target = jnp.minimum(c + COMBINE_SLOTS, total)

        def issue(k, carry2):
            chunk_copy(k).start()
            return carry2
        lax.fori_loop(issued, target, issue, 0)
        prog[0] = jnp.maximum(issued, target)

        @pl.when(prog[1] <= c)
        def _():
            chunk_copy(c).wait()
            prog[1] = c + 1

        posn = (c * STEP + lane).astype(F32)
        onehot = jnp.where((posn >= first) & (posn < end), 1.0, 0.0).astype(BF16)
        y = ybuf[c & (COMBINE_SLOTS - 1)]
        hi = y.astype(BF16)
        lo = (y - hi.astype(F32)).astype(BF16)
        o_ref[0] += _dot(onehot, hi) + _dot(onehot, lo)
        return carry
    lax.fori_loop(0, nchunks, body, 0)

    acc = o_ref[0]
    rr = acc * lax.rsqrt(jnp.mean(acc * acc, axis=-1, keepdims=True) + NORM_EPS) * gpost_ref[...]
    o_ref[0] = x1_ref[0] + mod_ref[0, 5:6, :] * rr


def _combine(off_flat, ytm, tokcol, x1, mod, g_post2):
    nb, n, d = x1.shape
    nt = n // STEP
    rows_pb = ytm.shape[0] // nb
    mod_map = (lambda b, i, *_: (b, 0, 0)) if mod.shape[0] == nb else (lambda b, i, *_: (0, 0, 0))
    grid_spec = pltpu.PrefetchScalarGridSpec(
        num_scalar_prefetch=1,
        grid=(nb, nt),
        in_specs=[pl.BlockSpec(memory_space=pl.ANY),
                  pl.BlockSpec((1, STEP, 128), lambda b, i, *_: (b, i, 0)),
                  pl.BlockSpec((1, STEP, d), lambda b, i, *_: (b, i, 0)),
                  pl.BlockSpec((1, 6, d), mod_map),
                  pl.BlockSpec((1, d), lambda b, i, *_: (0, 0))],
        out_specs=pl.BlockSpec((1, STEP, d), lambda b, i, *_: (b, i, 0)),
        scratch_shapes=[pltpu.VMEM((COMBINE_SLOTS, STEP, d), F32), pltpu.SemaphoreType.DMA((COMBINE_SLOTS,)),
                        pltpu.SMEM((2,), I32)],
    )
    return pl.pallas_call(
        functools.partial(_combine_body, rows_pb=rows_pb, nt=nt, nb=nb),
        grid_spec=grid_spec,
        out_shape=jax.ShapeDtypeStruct((nb, n, d), F32),
        compiler_params=_cparams(("arbitrary", "arbitrary"), 40 * 1024 * 1024),
        name="moe_combine",
    )(off_flat, ytm, tokcol, x1, mod, g_post2.reshape(1, d))


ROUTE_BISECTIONS = 24


def _route_body(aff_ref, code_ref, gate_ref, tokcol_ref, off_ref, slot_scr, sel_scr, ph_scr, pl_scr, gh_scr,
                gm_scr, gl_scr, *, n, cap):
    n_exp = N_EXPERTS
    nblk = n // STEP
    aff = aff_ref[0]
    ones_where = lambda m: jnp.where(m, 1.0, 0.0).astype(F32)

    thr = jnp.zeros((n_exp, 1), I32)
    for bit in range(30, -1, -1):
        cand = thr | (1 << bit)
        enough = jnp.sum(ones_where(aff >= pltpu.bitcast(cand, F32)), axis=1, keepdims=True) >= cap
        thr = jnp.where(enough, cand, thr)
    lo = pltpu.bitcast(thr, F32)
    hi = pltpu.bitcast(thr + 1, F32)
    for _ in range(ROUTE_BISECTIONS):
        mid = 0.5 * (lo + hi)
        enough = jnp.sum(ones_where(aff >= mid), axis=1, keepdims=True) >= cap
        lo = jnp.where(enough, mid, lo)
        hi = jnp.where(enough, hi, mid)
    gt = aff >= hi
    eq = (aff >= lo) & (aff < hi)
    need = cap - jnp.sum(ones_where(gt), axis=1, keepdims=True)
    tl = lax.broadcasted_iota(I32, (n_exp, n), 1)
    last = jnp.zeros((n_exp, 1), I32)
    for bit in range(n.bit_length() - 1, -1, -1):
        cand = last | (1 << bit)
        c = jnp.sum(ones_where(eq & (tl < cand)), axis=1, keepdims=True)
        last = jnp.where(c < need, cand, last)
    sel = ones_where(gt | (eq & (tl <= last) & (need > 0)))

    r16 = lax.broadcasted_iota(I32, (n_exp, n_exp), 0)
    c16 = lax.broadcasted_iota(I32, (n_exp, n_exp), 1)
    rank = _dot(jnp.where(c16 < r16, 1.0, 0.0).astype(BF16), sel.astype(BF16))
    cnt = jnp.sum(sel, axis=0, keepdims=True)

    rt = lax.broadcasted_iota(I32, (STEP, STEP), 0)
    ct = lax.broadcasted_iota(I32, (STEP, STEP), 1)
    upper = jnp.where(rt <= ct, 1.0, 0.0).astype(BF16)
    stacked = jnp.concatenate([sel, jnp.broadcast_to(cnt, (SUBLANES, n))], axis=0)
    base = jnp.zeros((n_exp + SUBLANES, 1), F32)
    lane = lax.broadcasted_iota(I32, (1, 128), 1)
    row8 = lax.broadcasted_iota(I32, (SUBLANES, STEP), 0)
    offs = jnp.zeros((1, 128), F32)
    for j in range(nblk):
        blk = slice(j * STEP, (j + 1) * STEP)
        incl = _dot(stacked[:, blk].astype(BF16), upper) + base
        offs = jnp.where(lane == j, base[n_exp:n_exp + 1, :], offs)
        base = incl[:, STEP - 1:STEP]
        cnt_b = cnt[:, blk]
        first_b = incl[n_exp:n_exp + 1, :] - cnt_b
        pos_b = first_b + rank[:, blk]
        pos_hi = jnp.floor(pos_b * (1.0 / STEP))
        slot_scr[:, blk] = incl[:n_exp, :] - 1.0
        sel_scr[:, blk] = sel[:, blk]
        ph_scr[:, blk] = pos_hi
        pl_scr[:, blk] = pos_b - pos_hi * STEP
        g = aff[:, blk]
        gh = g.astype(BF16).astype(F32)
        gm = (g - gh).astype(BF16).astype(F32)
        gh_scr[:, blk] = gh
        gm_scr[:, blk] = gm
        gl_scr[:, blk] = g - gh - gm
        cols = jnp.where(row8 == 0, first_b, jnp.where(row8 == 1, cnt_b, 0.0))
        tokcol_ref[0, blk, :] = jnp.concatenate([cols, jnp.zeros((STEP - SUBLANES, STEP), F32)], axis=0).T
    offs = jnp.where(lane == nblk, base[n_exp:n_exp + 1, :], offs)
    off_ref[0] = offs.astype(I32)

    s_iota = lax.broadcasted_iota(I32, (cap, STEP), 0).astype(F32)
    tok_lo = lax.broadcasted_iota(I32, (1, STEP), 1).astype(F32)

    code_ref[0] = jnp.zeros((n_exp, cap), I32)
    gate_ref[0] = jnp.zeros((n_exp, cap), F32)
    rowc = lax.broadcasted_iota(I32, (SUBLANES, cap), 0)

    def per_expert(e, carry):
        e8 = pl.multiple_of((e >> 3) << 3, SUBLANES)
        mine8 = row8 == (e & 7)
        acc = jnp.zeros((SUBLANES, cap), F32)
        for j in range(nblk):
            blk = slice(j * STEP, (j + 1) * STEP)
            row = lambda scr: jnp.sum(jnp.where(mine8, scr[pl.ds(e8, SUBLANES), blk], 0.0), axis=0, keepdims=True)
            onehot = jnp.where((row(slot_scr) == s_iota) & (row(sel_scr) > 0.0), 1.0, 0.0).astype(BF16)
            vals = jnp.where(row8 == 0, float(j), jnp.where(row8 == 1, tok_lo, jnp.where(
                row8 == 2, row(ph_scr), jnp.where(row8 == 3, row(pl_scr), jnp.where(
                    row8 == 4, row(gh_scr), jnp.where(row8 == 5, row(gm_scr), jnp.where(
                        row8 == 6, row(gl_scr), 0.0)))))))
            acc = acc + _dot_nt(vals.astype(BF16), onehot)
        ints = acc[0:4, :].astype(I32)
        token = ints[0:1] * STEP + ints[1:2]
        position = ints[2:3] * STEP + ints[3:4]
        minec = rowc == (e & 7)
        code_ref[0, pl.ds(e8, SUBLANES), :] = jnp.where(minec, (position << TOK_BITS) | token,
                                                        code_ref[0, pl.ds(e8, SUBLANES), :])
        gate_ref[0, pl.ds(e8, SUBLANES), :] = jnp.where(minec, (acc[4:5] + acc[5:6]) + acc[6:7],
                                                        gate_ref[0, pl.ds(e8, SUBLANES), :])
        return carry
    lax.fori_loop(0, n_exp, per_expert, 0)


def _route(aff_t, cap):
    b, n_exp, n = aff_t.shape
    assert n <= 1 << TOK_BITS and n // STEP < 128
    scr = pltpu.VMEM((n_exp, n), F32)
    return pl.pallas_call(
        functools.partial(_route_body, n=n, cap=cap),
        grid=(b,),
        in_specs=[pl.BlockSpec((1, n_exp, n), lambda bi: (bi, 0, 0))],
        out_specs=[pl.BlockSpec((1, n_exp, cap), lambda bi: (bi, 0, 0)),
                   pl.BlockSpec((1, n_exp, cap), lambda bi: (bi, 0, 0)),
                   pl.BlockSpec((1, n, 128), lambda bi: (bi, 0, 0)),
                   pl.BlockSpec((1, 1, 128), lambda bi: (bi, 0, 0))],
        out_shape=[jax.ShapeDtypeStruct((b, n_exp, cap), I32), jax.ShapeDtypeStruct((b, n_exp, cap), F32),
                   jax.ShapeDtypeStruct((b, n, 128), F32), jax.ShapeDtypeStruct((b, 1, 128), I32)],
        scratch_shapes=[scr] * 7,
        compiler_params=_cparams(("arbitrary",), 40 * 1024 * 1024),
        name="moe_route",
    )(aff_t)


def _moe(aff_t, h2, x1, mod, g_post2, w1, w3, w2, layer):
    b, n, d = x1.shape
    cap = EC_CAPACITY * n // N_EXPERTS
    code, gate, tokcol, off = _route(aff_t, cap)
    ytm = _expert_ffn(code.reshape(-1), h2, gate[..., None], w1, w3, w2, layer)
    rows = b * N_EXPERTS * cap
    return _combine(off[:, 0, :n // STEP + 1].reshape(-1), ytm.reshape(rows, d), tokcol, x1, mod, g_post2)


def _rope_tables(n):
    rows = n // GRID_W
    r = jnp.repeat(jnp.arange(rows, dtype=F32), GRID_W)
    col = jnp.tile(jnp.arange(GRID_W, dtype=F32), rows)
    half = HEAD_DIM // 2
    inv = ROPE_BASE ** (-jnp.arange(0, half, 2, dtype=F32) / half)
    ar, ac = r[:, None] * inv, col[:, None] * inv
    cr, sr, cc, sc = jnp.cos(ar), jnp.sin(ar), jnp.cos(ac), jnp.sin(ac)
    return (jnp.concatenate([cr, cr, cc, cc], axis=-1), jnp.concatenate([-sr, sr, -sc, sc], axis=-1))


def _prep_w_in(w_in):
    d = w_in.shape[0]
    main = w_in[:, :QKV_W + REST_GATE]
    gcols = w_in[:, QKV_W + REST_GATE:].reshape(d, 2, 2, DN_HEADS).transpose(0, 2, 1, 3).reshape(d, 4 * DN_HEADS)
    pad = jnp.zeros((d, IN_PAD_W - QKV_W - REST_GATE - 4 * DN_HEADS), w_in.dtype)
    return jnp.concatenate([main, gcols, pad], axis=1).astype(BF16)


def _lane_row(v8):
    return jnp.concatenate([v8.reshape(-1).astype(F32), jnp.zeros((128 - v8.size,), F32)]).reshape(1, 128)


def kernel(x, c, ctx, c_ctx, g_pre1, g_post1, g_pre2, g_post2, w_mod, b_mod, w_in, attn_sink, gm_ln, gm_ws, gm_bs,
           dn_conv, dn_a_log, dn_dt_bias, dn_norm, w_out, w_router, w_e1, w_e3, w_e2):
    b, n, d = x.shape
    depth = w_in.shape[0]
    rope_cs = _rope_tables(n)
    c8 = jnp.concatenate([c, c_ctx[None, :], jnp.zeros((8 - b - 1, d), F32)], axis=0)
    w1, w3, w2 = w_e1.astype(BF16), w_e3.astype(BF16), w_e2.astype(BF16)
    xc = ctx
    for l in range(depth):
        update_ctx = l < depth - 1
        mod_all = _modulation(c8, w_mod, b_mod[l], l)
        mod_lat = mod_all[:b].reshape(b, 6, d)
        mod_ctx = mod_all[b:b + 1].reshape(1, 6, d)
        w_in_bf = _prep_w_in(w_in[l])
        qkv, rest = _inproj(x, mod_lat, g_pre1[l], w_in_bf, rope_cs)
        qkvc, restc = _inproj(xc, mod_ctx, g_pre1[l], w_in_bf, None)
        ya = _attention(qkv, qkvc, attn_sink[l])
        ws_bf = gm_ws[l].astype(BF16)
        yg = _gmlp(rest, gm_ln[l], ws_bf, gm_bs[l])
        (o_f, o_b), (oc_f, oc_b) = _deltanet(rest, restc, _lane_row(dn_a_log[l]), _lane_row(dn_dt_bias[l]),
                                             dn_conv[l])
        w_out_bf = w_out[l].astype(BF16)
        w_router_t = w_router[l].T
        x1, h2, aff_t = _outproj(ya, yg, o_f, o_b, rest, x, mod_lat, dn_norm[l], w_out_bf, g_post1[l], g_pre2[l],
                                 w_router_t)
        x = _moe(aff_t, h2, x1, mod_lat, g_post2[l], w1, w3, w2, l)
        if update_ctx:
            yac = _ctx_attention(qkvc, attn_sink[l])
            ygc = _gmlp(restc, gm_ln[l], ws_bf, gm_bs[l])
            xc1, h2c, affc_t = _outproj(yac, ygc, oc_f, oc_b, restc, xc, mod_ctx, dn_norm[l], w_out_bf, g_post1[l],
                                        g_pre2[l], w_router_t)
            xc = _moe(affc_t, h2c, xc1, mod_ctx, g_post2[l], w1, w3, w2, l)
    return x
```

```python
import functools

import jax
import jax.numpy as jnp
from jax import lax
from jax.experimental import pallas as pl
from jax.experimental.pallas import tpu as pltpu

F32 = jnp.float32
BF16 = jnp.bfloat16
I32 = jnp.int32

HEAD_DIM = 128
GRID_W = 64
ATTN_HEADS = 8
ATTN_KV_HEADS = 2
ATTN_GROUP = ATTN_HEADS // ATTN_KV_HEADS
WINDOW = 128
ATTN_BLOCK = 128
ROPE_BASE = 10000.0
MASK_VALUE = -1e30
GM_HEADS = 4
GM_CHUNK = 128
DN_HEADS = 4
DN_CHUNK = 64
N_EXPERTS = 16
EC_CAPACITY = 2
NORM_EPS = 1e-6

ATTN_W = ATTN_HEADS * HEAD_DIM
KV_W = ATTN_KV_HEADS * HEAD_DIM
QKV_W = ATTN_W + 2 * KV_W
GM_W = GM_HEADS * HEAD_DIM
DN_W = DN_HEADS * HEAD_DIM
MIX_W = ATTN_W + GM_W + DN_W
REST_GU, REST_GV, REST_DQ, REST_DK, REST_DV, REST_DZ, REST_GATE = 0, 512, 1024, 1536, 2048, 2560, 3072
REST_W = 3200
IN_PAD_W = QKV_W + REST_W

V7X_VMEM_LIMIT = 56 * 1024 * 1024
STEP = 128


def _cparams(sem, vmem=None):
    return pltpu.CompilerParams(dimension_semantics=sem, vmem_limit_bytes=vmem)


def _silu(x):
    return x * jax.nn.sigmoid(x)


def _dot(a, b):
    return jnp.dot(a, b, preferred_element_type=F32)


def _dot_nt(a, b):
    return lax.dot_general(a, b, (((1,), (1,)), ((), ())), preferred_element_type=F32)


def _split2(a):
    hi = a.astype(BF16)
    lo = (a - hi.astype(F32)).astype(BF16)
    return hi, lo


def _split3(a):
    hi = a.astype(BF16)
    r = a - hi.astype(F32)
    mid = r.astype(BF16)
    lo = (r - mid.astype(F32)).astype(BF16)
    return hi, mid, lo


def _dot_x3(a, b, nt=False):
    d = _dot_nt if nt else _dot
    ah, al = _split2(a)
    bh, bl = _split2(b)
    return d(ah, bh) + d(ah, bl) + d(al, bh)


def _mod_body(c_ref, w_ref, b_ref, o_ref):
    s = _silu(c_ref[...]).astype(BF16)
    o_ref[...] = _dot(s, w_ref[0].astype(BF16)) + b_ref[...]


def _modulation(c8, w_mod_all, b_mod, layer):
    _, d, n6 = w_mod_all.shape
    tn = 1024
    return pl.pallas_call(
        _mod_body,
        grid=(n6 // tn,),
        in_specs=[pl.BlockSpec((8, d), lambda j: (0, 0)),
                  pl.BlockSpec((1, d, tn), lambda j: (layer, 0, j)),
                  pl.BlockSpec((1, tn), lambda j: (0, j))],
        out_specs=pl.BlockSpec((8, tn), lambda j: (0, j)),
        out_shape=jax.ShapeDtypeStruct((8, n6), F32),
        compiler_params=_cparams(("arbitrary",), 40 * 1024 * 1024),
        name="modulation",
    )(c8, w_mod_all, b_mod.reshape(1, n6))


def _inproj_body(*refs, rope, tm):
    if rope:
        x_ref, mod_ref, g_ref, w_ref, cos_ref, sin_ref, qkv_ref, rest_ref = refs
    else:
        x_ref, mod_ref, g_ref, w_ref, qkv_ref, rest_ref = refs
    x = x_ref[0]
    ms = jnp.mean(x * x, axis=-1, keepdims=True)
    y = x * lax.rsqrt(ms + NORM_EPS) * g_ref[...]
    h = y * (1.0 + mod_ref[0, 1:2, :]) + mod_ref[0, 0:1, :]
    hb = h.astype(BF16)
    if rope:
        cosf = cos_ref[...]
        sinf = sin_ref[...]
        lane = lax.broadcasted_iota(I32, (tm, HEAD_DIM), 1)
        first = (lane & 32) == 0

    def rot(t):
        if not rope:
            return t
        partner = jnp.where(first, pltpu.roll(t, 96, 1), pltpu.roll(t, 32, 1))
        return t * cosf + partner * sinf

    for c0 in (0, 512):
        t = _dot(hb, w_ref[:, c0:c0 + 512])
        for j in range(4):
            th = rot(t[:, j * 128:(j + 1) * 128]) * (HEAD_DIM ** -0.5)
            qkv_ref[0, :, c0 + j * 128:c0 + (j + 1) * 128] = th.astype(BF16)
    t = _dot(hb, w_ref[:, ATTN_W:ATTN_W + 512])
    for j in range(2):
        qkv_ref[0, :, ATTN_W + j * 128:ATTN_W + (j + 1) * 128] = rot(t[:, j * 128:(j + 1) * 128]).astype(BF16)
    qkv_ref[0, :, ATTN_W + KV_W:QKV_W] = t[:, 256:512].astype(BF16)
    for c0 in range(0, REST_W, 512):
        cw = min(512, REST_W - c0)
        rest_ref[0, :, c0:c0 + cw] = _dot(hb, w_ref[:, QKV_W + c0:QKV_W + c0 + cw])


def _inproj(x, mod, g, w_bf, rope_cs):
    b, t, d = x.shape
    tm = 256
    rope = rope_cs is not None
    mod_map = (lambda bi, i: (bi, 0, 0)) if mod.shape[0] == b else (lambda bi, i: (0, 0, 0))
    in_specs = [pl.BlockSpec((1, tm, d), lambda bi, i: (bi, i, 0)),
                pl.BlockSpec((1, 6, d), mod_map),
                pl.BlockSpec((1, d), lambda bi, i: (0, 0)),
                pl.BlockSpec((d, IN_PAD_W), lambda bi, i: (0, 0), pipeline_mode=pl.Buffered(1))]
    args = [x, mod, g.reshape(1, d), w_bf]
    if rope:
        in_specs += [pl.BlockSpec((tm, HEAD_DIM), lambda bi, i: (i, 0)),
                     pl.BlockSpec((tm, HEAD_DIM), lambda bi, i: (i, 0))]
        args += list(rope_cs)
    return pl.pallas_call(
        functools.partial(_inproj_body, rope=rope, tm=tm),
        grid=(b, t // tm),
        in_specs=in_specs,
        out_specs=[pl.BlockSpec((1, tm, QKV_W), lambda bi, i: (bi, i, 0)),
                   pl.BlockSpec((1, tm, REST_W), lambda bi, i: (bi, i, 0))],
        out_shape=[jax.ShapeDtypeStruct((b, t, QKV_W), BF16),
                   jax.ShapeDtypeStruct((b, t, REST_W), F32)],
        compiler_params=_cparams(("parallel", "arbitrary"), V7X_VMEM_LIMIT),
        name="inproj_rope" if rope else "inproj",
    )(*args)


def _group_queries(q_ref, rows, kh):
    return jnp.concatenate(
        [q_ref[0, rows, (kh * ATTN_GROUP + g) * HEAD_DIM:(kh * ATTN_GROUP + g + 1) * HEAD_DIM]
         for g in range(ATTN_GROUP)], axis=0)


def _softmax_pv_blocks(sink_ref, o_ref, blocks, scores, values):
    probs, dens = {}, {}
    for blk in blocks:
        _, kh = blk
        sink_col = jnp.concatenate(
            [jnp.full((ATTN_BLOCK, 1), sink_ref[kh * ATTN_GROUP + g], F32) for g in range(ATTN_GROUP)], axis=0)
        mx = sink_col
        for s in scores[blk]:
            mx = jnp.maximum(mx, jnp.max(s, axis=-1, keepdims=True))
        probs[blk] = [jnp.exp(s - mx).astype(BF16) for s in scores[blk]]
        dens[blk] = jnp.exp(sink_col - mx)
    for blk in blocks:
        j, kh = blk
        acc = None
        for e, v in zip(probs[blk], values[blk]):
            pv = _dot(e, jnp.concatenate([v, jnp.ones_like(v)], axis=1))
            acc = pv if acc is None else acc + pv
        o = acc[:, :HEAD_DIM] / (dens[blk] + acc[:, HEAD_DIM:HEAD_DIM + 1])
        rows = slice(j * ATTN_BLOCK, (j + 1) * ATTN_BLOCK)
        for g in range(ATTN_GROUP):
            hh = kh * ATTN_GROUP + g
            o_ref[0, rows, hh * HEAD_DIM:(hh + 1) * HEAD_DIM] = o[g * ATTN_BLOCK:(g + 1) * ATTN_BLOCK].astype(BF16)


def _attn_body(sink_ref, q_ref, km_ref, vm_ref, kp_ref, vp_ref, kn_ref, vn_ref, kc_ref, vc_ref, o_ref, *, tq, n):
    i = pl.program_id(1)
    qb = tq // ATTN_BLOCK
    qi = lax.broadcasted_iota(I32, (ATTN_BLOCK, 3 * ATTN_BLOCK), 0)
    sj = lax.broadcasted_iota(I32, (ATTN_BLOCK, 3 * ATTN_BLOCK), 1) - ATTN_BLOCK

    def band(main_ref, prev_ref, next_ref, j, hs):
        pieces = []
        for blk in (j - 1, j, j + 1):
            if blk < 0:
                pieces.append(prev_ref[0, :, hs])
            elif blk >= qb:
                pieces.append(next_ref[0, :, hs])
            else:
                pieces.append(main_ref[0, blk * ATTN_BLOCK:(blk + 1) * ATTN_BLOCK, hs])
        return jnp.concatenate(pieces, axis=0)

    blocks = [(j, kh) for j in range(qb) for kh in range(ATTN_KV_HEADS)]
    scores, values = {}, {}
    for j in range(qb):
        kpos = (i * qb + j) * ATTN_BLOCK + sj
        ok = (jnp.abs(sj - qi) <= WINDOW) & (kpos >= 0) & (kpos < n)
        bias = jnp.where(ok, 0.0, MASK_VALUE).astype(F32)
        bias4 = jnp.concatenate([bias] * ATTN_GROUP, axis=0)
        rows = slice(j * ATTN_BLOCK, (j + 1) * ATTN_BLOCK)
        for kh in range(ATTN_KV_HEADS):
            hs = slice(kh * HEAD_DIM, (kh + 1) * HEAD_DIM)
            q = _group_queries(q_ref, rows, kh)
            scores[(j, kh)] = [_dot_nt(q, band(km_ref, kp_ref, kn_ref, j, hs)) + bias4,
                               _dot_nt(q, kc_ref[0, :, hs])]
            values[(j, kh)] = [band(vm_ref, vp_ref, vn_ref, j, hs), vc_ref[0, :, hs]]
    _softmax_pv_blocks(sink_ref, o_ref, blocks, scores, values)


def _attention(qkv, qkvc, sink):
    b, n, _ = qkv.shape
    m = qkvc.shape[1]
    tq = 512
    qb = tq // ATTN_BLOCK
    nb = n // ATTN_BLOCK
    kcol, vcol = ATTN_W // KV_W, ATTN_W // KV_W + 1
    smem = pl.BlockSpec(memory_space=pltpu.SMEM)
    main = lambda col: pl.BlockSpec((1, tq, KV_W), lambda bi, i: (bi, i, col))
    prev = lambda col: pl.BlockSpec((1, ATTN_BLOCK, KV_W), lambda bi, i: (bi, jnp.maximum(i * qb - 1, 0), col))
    nxt = lambda col: pl.BlockSpec((1, ATTN_BLOCK, KV_W), lambda bi, i: (bi, jnp.minimum(i * qb + qb, nb - 1), col))
    ctx = lambda col: pl.BlockSpec((1, m, KV_W), lambda bi, i: (bi, 0, col))
    return pl.pallas_call(
        functools.partial(_attn_body, tq=tq, n=n),
        grid=(b, n // tq),
        in_specs=[smem, pl.BlockSpec((1, tq, ATTN_W), lambda bi, i: (bi, i, 0)),
                  main(kcol), main(vcol), prev(kcol), prev(vcol), nxt(kcol), nxt(vcol), ctx(kcol), ctx(vcol)],
        out_specs=pl.BlockSpec((1, tq, ATTN_W), lambda bi, i: (bi, i, 0)),
        out_shape=jax.ShapeDtypeStruct((b, n, ATTN_W), BF16),
        compiler_params=_cparams(("parallel", "arbitrary"), 40 * 1024 * 1024),
        name="window_attention",
    )(sink, qkv, qkv, qkv, qkv, qkv, qkv, qkv, qkvc, qkvc)


def _ctx_attn_body(sink_ref, q_ref, kc_ref, vc_ref, o_ref, *, m):
    blocks = [(j, kh) for j in range(m // ATTN_BLOCK) for kh in range(ATTN_KV_HEADS)]
    scores, values = {}, {}
    for j, kh in blocks:
        rows = slice(j * ATTN_BLOCK, (j + 1) * ATTN_BLOCK)
        hs = slice(kh * HEAD_DIM, (kh + 1) * HEAD_DIM)
        scores[(j, kh)] = [_dot_nt(_group_queries(q_ref, rows, kh), kc_ref[0, :, hs])]
        values[(j, kh)] = [vc_ref[0, :, hs]]
    _softmax_pv_blocks(sink_ref, o_ref, blocks, scores, values)


def _ctx_attention(qkvc, sink):
    b, m, _ = qkvc.shape
    kcol, vcol = ATTN_W // KV_W, ATTN_W // KV_W + 1
    return pl.pallas_call(
        functools.partial(_ctx_attn_body, m=m),
        grid=(b,),
        in_specs=[pl.BlockSpec(memory_space=pltpu.SMEM),
                  pl.BlockSpec((1, m, ATTN_W), lambda bi: (bi, 0, 0)),
                  pl.BlockSpec((1, m, KV_W), lambda bi: (bi, 0, kcol)),
                  pl.BlockSpec((1, m, KV_W), lambda bi: (bi, 0, vcol))],
        out_specs=pl.BlockSpec((1, m, ATTN_W), lambda bi: (bi, 0, 0)),
        out_shape=jax.ShapeDtypeStruct((b, m, ATTN_W), BF16),
        compiler_params=_cparams(("parallel",), 40 * 1024 * 1024),
        name="context_attention",
    )(sink, qkvc, qkvc, qkvc)


def _gmlp_body(u_ref, v_ref, ln_ref, ws_ref, bst_ref, o_ref, *, tg):
    for ci in range(tg // GM_CHUNK):
        rows = slice(ci * GM_CHUNK, (ci + 1) * GM_CHUNK)
        u = jax.nn.gelu(u_ref[0, rows, :])
        v = jax.nn.gelu(v_ref[0, rows, :])
        for h in range(GM_HEADS):
            hs = slice(h * HEAD_DIM, (h + 1) * HEAD_DIM)
            vh = v[:, hs]
            vh = vh - jnp.mean(vh, axis=-1, keepdims=True)
            vh = vh * lax.rsqrt(jnp.mean(vh * vh, axis=-1, keepdims=True) + NORM_EPS) * ln_ref[:, hs]
            mixed = _dot(ws_ref[h], vh.astype(BF16)) + bst_ref[:, h:h + 1]
            o_ref[0, rows, hs] = (u[:, hs] * mixed).astype(BF16)


def _gmlp(rest, gm_ln, gm_ws_bf, gm_bs):
    b, n, _ = rest.shape
    tg = min(512, n)
    return pl.pallas_call(
        functools.partial(_gmlp_body, tg=tg),
        grid=(b, n // tg),
        in_specs=[pl.BlockSpec((1, tg, GM_W), lambda bi, i: (bi, i, REST_GU // GM_W)),
                  pl.BlockSpec((1, tg, GM_W), lambda bi, i: (bi, i, REST_GV // GM_W)),
                  pl.BlockSpec((1, GM_W), lambda bi, i: (0, 0)),
                  pl.BlockSpec((GM_HEADS, GM_CHUNK, GM_CHUNK), lambda bi, i: (0, 0, 0)),
                  pl.BlockSpec((GM_CHUNK, GM_HEADS), lambda bi, i: (0, 0))],
        out_specs=pl.BlockSpec((1, tg, GM_W), lambda bi, i: (bi, i, 0)),
        out_shape=jax.ShapeDtypeStruct((b, n, GM_W), BF16),
        compiler_params=_cparams(("parallel", "arbitrary"), 40 * 1024 * 1024),
        name="gmlp",
    )(rest, rest, gm_ln.reshape(1, GM_W), gm_ws_bf, gm_bs.T)


def _dn_pre_body(alog_ref, dtb_ref, cw_ref, q_ref, k_ref, v_ref, qp_ref, kp_ref, vp_ref,
                 qn_ref, kn_ref, vn_ref, gate_ref, oq_ref, ok_ref, ov_ref, g_ref, gt_ref, *, td, nt):
    i = pl.program_id(1)
    row = lax.broadcasted_iota(I32, (td, DN_W), 0)
    has_prev = (i > 0).astype(F32)
    has_next = (i < nt - 1).astype(F32)

    def conv(x_ref, p_ref, n_ref, c0):
        x = x_ref[0]
        prev_row = p_ref[0, 7:8, :] * has_prev
        next_row = n_ref[0, 0:1, :] * has_next
        xm = jnp.where(row == 0, prev_row, pltpu.roll(x, 1, 0))
        xp = jnp.where(row == td - 1, next_row, pltpu.roll(x, td - 1, 0))
        y = (cw_ref[0:1, c0:c0 + DN_W] * xm + cw_ref[1:2, c0:c0 + DN_W] * x
             + cw_ref[2:3, c0:c0 + DN_W] * xp)
        return _silu(y)

    def l2n(y, h):
        yh = y[:, h * HEAD_DIM:(h + 1) * HEAD_DIM]
        return yh * lax.rsqrt(jnp.sum(yh * yh, axis=-1, keepdims=True) + NORM_EPS)

    yq = conv(q_ref, qp_ref, qn_ref, 0)
    yk = conv(k_ref, kp_ref, kn_ref, DN_W)
    for h in range(DN_HEADS):
        hs = slice(h * HEAD_DIM, (h + 1) * HEAD_DIM)
        oq_ref[0, :, hs] = l2n(yq, h) * (HEAD_DIM ** -0.5)
        ok_ref[0, :, hs] = l2n(yk, h)
    ov_ref[0] = conv(v_ref, vp_ref, vn_ref, 2 * DN_W)

    raw = gate_ref[0]
    z = raw + dtb_ref[...]
    softplus = jnp.maximum(z, 0.0) + jnp.log(1.0 + jnp.exp(-jnp.abs(z)))
    gval = -jnp.exp(alog_ref[...]) * softplus
    lane = lax.broadcasted_iota(I32, (td, 128), 1)
    gates = jnp.where(lane < 2 * DN_HEADS, gval, jax.nn.sigmoid(raw))
    g_ref[0] = gates
    gt_ref[0] = gates.T[0:16, :]


def _dn_pre(rest, alog_row, dtb_row, conv_w):
    b, n, _ = rest.shape
    td = 256
    nt = n // td
    cur = lambda col: pl.BlockSpec((1, td, DN_W), lambda bi, i: (bi, i, col))
    prev = lambda col: pl.BlockSpec((1, 8, DN_W), lambda bi, i: (bi, jnp.maximum(i * (td // 8) - 1, 0), col))
    nxt = lambda col: pl.BlockSpec((1, 8, DN_W), lambda bi, i: (bi, jnp.minimum((i + 1) * (td // 8), n // 8 - 1), col))
    cq, ck, cv = REST_DQ // DN_W, REST_DK // DN_W, REST_DV // DN_W
    row128 = pl.BlockSpec((1, 128), lambda bi, i: (0, 0))
    tok = lambda w: pl.BlockSpec((1, td, w), lambda bi, i: (bi, i, 0))
    return pl.pallas_call(
        functools.partial(_dn_pre_body, td=td, nt=nt),
        grid=(b, nt),
        in_specs=[row128, row128, pl.BlockSpec((3, 3 * DN_W), lambda bi, i: (0, 0)),
                  cur(cq), cur(ck), cur(cv), prev(cq), prev(ck), prev(cv), nxt(cq), nxt(ck), nxt(cv),
                  pl.BlockSpec((1, td, 128), lambda bi, i: (bi, i, REST_GATE // 128))],
        out_specs=[tok(DN_W), tok(DN_W), tok(DN_W), tok(128),
                   pl.BlockSpec((1, 16, td), lambda bi, i: (bi, 0, i))],
        out_shape=[jax.ShapeDtypeStruct((b, n, DN_W), F32)] * 3
        + [jax.ShapeDtypeStruct((b, n, 128), F32), jax.ShapeDtypeStruct((b, 16, n), F32)],
        compiler_params=_cparams(("parallel", "arbitrary"), 40 * 1024 * 1024),
        name="deltanet_pre",
    )(alog_row, dtb_row, conv_w, rest, rest, rest, rest, rest, rest, rest, rest, rest, rest)


def _dn_chunk_body(q_ref, k_ref, v_ref, g_ref, gt_ref, *out_refs, nb):
    dir_refs = (out_refs[:6], out_refs[6:])
    r = lax.broadcasted_iota(I32, (STEP, STEP), 0)
    c = lax.broadcasted_iota(I32, (STEP, STEP), 1)
    same = (r // DN_CHUNK) == (c // DN_CHUNK)
    incl = (same & (c <= r), same & (c >= r))
    strict = (same & (c < r), same & (c > r))
    lower_b = jnp.where(incl[0], 1.0, 0.0).astype(BF16)
    upper_b = jnp.where(incl[1], 1.0, 0.0).astype(BF16)
    same_b = jnp.where(same, 1.0, 0.0).astype(BF16)
    lane = lax.broadcasted_iota(I32, (STEP, 128), 1)

    gates, gc_col, gc_row, tot_col, tot_row = [], [], [], [], []
    for bi in range(nb):
        g = g_ref[bi]
        gt = gt_ref[bi]
        gh, gm, gl = _split3(g)
        th, tm_, tl = _split3(gt)
        cf_col = _dot(lower_b, gh) + _dot(lower_b, gm) + _dot(lower_b, gl)
        tc = _dot(same_b, gh) + _dot(same_b, gm) + _dot(same_b, gl)
        cf_row = _dot(th, upper_b) + _dot(tm_, upper_b) + _dot(tl, upper_b)
        tr = _dot(th, same_b) + _dot(tm_, same_b) + _dot(tl, same_b)
        gates.append(g)
        gc_col.append((cf_col, tc - cf_col + g))
        gc_row.append((cf_row, tr - cf_row + gt))
        tot_col.append(tc)
        tot_row.append(tr)

    heads = [(bi, h) for bi in range(nb) for h in range(DN_HEADS)]
    hsl = lambda h: slice(h * HEAD_DIM, (h + 1) * HEAD_DIM)
    kbf = {bh: k_ref[bh[0], :, hsl(bh[1])].astype(BF16) for bh in heads}
    kk = {bh: _dot_nt(kbf[bh], kbf[bh]) for bh in heads}
    qk = {bh: _dot_nt(q_ref[bh[0], :, hsl(bh[1])].astype(BF16), kbf[bh]) for bh in heads}

    probs = [(bi, h, d) for bi, h in heads for d in range(2)]
    dec, y, p = {}, {}, {}
    for pr in probs:
        bi, h, d = pr
        lg = d * DN_HEADS + h
        lb = 2 * DN_HEADS + d * DN_HEADS + h
        gcc = gc_col[bi][d][:, lg:lg + 1]
        gcr = gc_row[bi][d][lg:lg + 1, :]
        dec[pr] = jnp.exp(jnp.where(incl[d], gcc - gcr, MASK_VALUE))
        y[pr] = -jnp.where(strict[d], kk[(bi, h)] * gates[bi][:, lb:lb + 1] * dec[pr], 0.0)
        p[pr] = y[pr]
    for _ in range(5):
        for pr in probs:
            pb = p[pr].astype(BF16)
            p[pr] = _dot(pb, pb)
        for pr in probs:
            y[pr] = y[pr] + p[pr] + _dot(y[pr].astype(BF16), p[pr].astype(BF16))

    for pr in probs:
        bi, h, d = pr
        hs = hsl(h)
        u_ref, w_ref, qg_ref, kgt_ref, intra_ref, _ = dir_refs[d]
        lg = d * DN_HEADS + h
        lb = 2 * DN_HEADS + d * DN_HEADS + h
        gcc = gc_col[bi][d][:, lg:lg + 1]
        gcr = gc_row[bi][d][lg:lg + 1, :]
        totr = tot_row[bi][lg:lg + 1, :]
        beta = gates[bi][:, lb:lb + 1]
        q = q_ref[bi, :, hs]
        k = k_ref[bi, :, hs]
        egc = jnp.exp(gcc)
        rhs = jnp.concatenate([v_ref[bi, :, hs] * beta, k * (beta * egc)], axis=1)
        sol = rhs + _dot(y[pr].astype(BF16), rhs.astype(BF16))
        u_ref[bi, :, hs] = sol[:, :HEAD_DIM]
        w_ref[bi, :, hs] = sol[:, HEAD_DIM:].astype(BF16)
        intra_ref[bi, :, hs] = (qk[(bi, h)] * dec[pr]).astype(BF16)
        qg_ref[bi, :, hs] = (q * egc).astype(BF16)
        kgt_ref[bi, hs, :] = (k.T * jnp.exp(totr - gcr)).astype(BF16)

    for bi in range(nb):
        for d in range(2):
            eg = jnp.zeros((STEP, 128), F32)
            for h in range(DN_HEADS):
                lg = d * DN_HEADS + h
                eg = jnp.where(lane == h, jnp.exp(tot_col[bi][:, lg:lg + 1]), eg)
            dir_refs[d][5][bi] = eg


DN_CHUNK_NB = 2


def _dn_chunk(dq, dk, dv, gates, gates_t):
    b, n, _ = dq.shape
    nb = DN_CHUNK_NB if b % DN_CHUNK_NB == 0 else 1
    tok = lambda w: pl.BlockSpec((nb, STEP, w), lambda bi, i: (bi, i, 0))
    one_dir_specs = [tok(DN_W), tok(DN_W), tok(DN_W),
                     pl.BlockSpec((nb, DN_W, STEP), lambda bi, i: (bi, 0, i)), tok(DN_W), tok(128)]
    one_dir_shapes = [jax.ShapeDtypeStruct((b, n, DN_W), F32), jax.ShapeDtypeStruct((b, n, DN_W), BF16),
                      jax.ShapeDtypeStruct((b, n, DN_W), BF16), jax.ShapeDtypeStruct((b, DN_W, n), BF16),
                      jax.ShapeDtypeStruct((b, n, DN_W), BF16), jax.ShapeDtypeStruct((b, n, 128), F32)]
    outs = pl.pallas_call(
        functools.partial(_dn_chunk_body, nb=nb),
        grid=(b // nb, n // STEP),
        in_specs=[tok(DN_W), tok(DN_W), tok(DN_W), tok(128),
                  pl.BlockSpec((nb, 16, STEP), lambda bi, i: (bi, 0, i))],
        out_specs=one_dir_specs * 2,
        out_shape=one_dir_shapes * 2,
        compiler_params=_cparams(("parallel", "arbitrary"), 48 * 1024 * 1024),
        name="deltanet_chunk",
    )(dq, dk, dv, gates, gates_t)
    return outs[:6], outs[6:]


def _dn_scan_body(*refs, nb, ns):
    dir_in = (refs[0:7], refs[7:14])
    o_refs = refs[14:16]
    sout_refs = refs[16:18]
    s_scr = refs[18]
    i = pl.program_id(0)

    @pl.when(i == 0)
    def _():
        s_scr[0] = dir_in[0][6][...]
        s_scr[1] = dir_in[1][6][...]

    zeros = jnp.zeros((DN_CHUNK, HEAD_DIM), F32)
    chains = [(d, bi, h) for d in range(2) for bi in range(nb) for h in range(DN_HEADS)]
    state = {ch: s_scr[ch[0], ch[1], ch[2]] for ch in chains}
    for pos in range(2):
        ws = {}
        for ch in chains:
            d, bi, h = ch
            ci = pos if d == 0 else 1 - pos
            rows = slice(ci * DN_CHUNK, (ci + 1) * DN_CHUNK)
            hs = slice(h * HEAD_DIM, (h + 1) * HEAD_DIM)
            wq = jnp.concatenate([dir_in[d][1][bi, rows, hs], dir_in[d][2][bi, rows, hs]], axis=0)
            ws[ch] = _dot(wq, state[ch].astype(BF16))
        for ch in chains:
            d, bi, h = ch
            u_ref, _, _, kgt_ref, intra_ref, eg_ref, _ = dir_in[d]
            ci = pos if d == 0 else 1 - pos
            rows = slice(ci * DN_CHUNK, (ci + 1) * DN_CHUNK)
            hs = slice(h * HEAD_DIM, (h + 1) * HEAD_DIM)
            v_new = u_ref[bi, rows, hs] - ws[ch][:DN_CHUNK]
            pieces = [v_new, zeros] if ci == 0 else [zeros, v_new]
            v_full = jnp.concatenate(pieces, axis=0).astype(BF16)
            o_refs[d][bi, rows, hs] = ws[ch][DN_CHUNK:] + _dot(intra_ref[bi, rows, hs], v_full)
            e = eg_ref[bi, rows, h:h + 1]
            state[ch] = state[ch] * jnp.concatenate([e, e], axis=0) + _dot(kgt_ref[bi, hs, :], v_full)
    for ch in chains:
        s_scr[ch[0], ch[1], ch[2]] = state[ch]

    @pl.when(i == ns - 1)
    def _():
        sout_refs[0][...] = s_scr[0]
        sout_refs[1][...] = s_scr[1]


def _dn_scan(fwd, bwd, s0_f, s0_b):
    b, n, _ = fwd[0].shape
    ns = n // STEP

    def specs(pos):
        tok = lambda wd: pl.BlockSpec((b, STEP, wd), lambda i: (0, pos(i), 0))
        return [tok(DN_W), tok(DN_W), tok(DN_W), pl.BlockSpec((b, DN_W, STEP), lambda i: (0, 0, pos(i))),
                tok(DN_W), tok(128), state]

    state = pl.BlockSpec((b, DN_HEADS, HEAD_DIM, HEAD_DIM), lambda i: (0, 0, 0, 0))
    fpos = lambda i: i
    bpos = lambda i: ns - 1 - i
    state_shape = jax.ShapeDtypeStruct((b, DN_HEADS, HEAD_DIM, HEAD_DIM), F32)
    return pl.pallas_call(
        functools.partial(_dn_scan_body, nb=b, ns=ns),
        grid=(ns,),
        in_specs=specs(fpos) + specs(bpos),
        out_specs=[pl.BlockSpec((b, STEP, DN_W), lambda i: (0, fpos(i), 0)),
                   pl.BlockSpec((b, STEP, DN_W), lambda i: (0, bpos(i), 0)), state, state],
        out_shape=[jax.ShapeDtypeStruct((b, n, DN_W), F32), jax.ShapeDtypeStruct((b, n, DN_W), F32),
                   state_shape, state_shape],
        scratch_shapes=[pltpu.VMEM((2, b, DN_HEADS, HEAD_DIM, HEAD_DIM), F32)],
        compiler_params=_cparams(("arbitrary",), 48 * 1024 * 1024),
        name="deltanet_scan",
    )(*fwd, s0_f, *bwd, s0_b)


def _deltanet(rest, restc, alog_row, dtb_row, conv_w):
    b = rest.shape[0]
    zero_state = jnp.zeros((b, DN_HEADS, HEAD_DIM, HEAD_DIM), F32)
    cf, cb = _dn_chunk(*_dn_pre(restc, alog_row, dtb_row, conv_w))
    oc_f, oc_b, sc_f, sc_b = _dn_scan(cf, cb, zero_state, zero_state)
    lf, lb = _dn_chunk(*_dn_pre(rest, alog_row, dtb_row, conv_w))
    ol_f, ol_b, _, _ = _dn_scan(lf, lb, sc_f, sc_b)
    return (ol_f, ol_b), (oc_f, oc_b)


OUTPROJ_SUB = 256


def _outproj_body(ya_ref, yg_ref, of_ref, ob_ref, z_ref, x_ref, mod_ref, dnn_ref, w_ref, gpost_ref, gpre2_ref,
                  wr_ref, x1_ref, h2_ref, aff_ref, *, tm):
    def project(rows):
        o = of_ref[0, rows, :] + ob_ref[0, rows, :]
        z = z_ref[0, rows, :]
        acc = _dot(ya_ref[0, rows, :], w_ref[0:ATTN_W, :]) + _dot(yg_ref[0, rows, :], w_ref[ATTN_W:ATTN_W + GM_W, :])
        for h in range(DN_HEADS):
            hs = slice(h * HEAD_DIM, (h + 1) * HEAD_DIM)
            oh = o[:, hs]
            yh = oh * lax.rsqrt(jnp.mean(oh * oh, axis=-1, keepdims=True) + NORM_EPS) * dnn_ref[...]
            yd = (yh * _silu(z[:, hs])).astype(BF16)
            r0 = ATTN_W + GM_W + h * HEAD_DIM
            acc = acc + _dot(yd, w_ref[r0:r0 + HEAD_DIM, :])
        return acc

    def finish(rows, acc):
        r = acc * lax.rsqrt(jnp.mean(acc * acc, axis=-1, keepdims=True) + NORM_EPS) * gpost_ref[...]
        x1 = x_ref[0, rows, :] + mod_ref[0, 2:3, :] * r
        x1_ref[0, rows, :] = x1
        y2 = x1 * lax.rsqrt(jnp.mean(x1 * x1, axis=-1, keepdims=True) + NORM_EPS) * gpre2_ref[...]
        h2 = y2 * (1.0 + mod_ref[0, 4:5, :]) + mod_ref[0, 3:4, :]
        h2_ref[0, rows, :] = h2
        logits_t = _dot_x3(wr_ref[...], h2, nt=True)
        e = jnp.exp(logits_t - jnp.max(logits_t, axis=0, keepdims=True))
        aff_ref[0, :, rows] = e / jnp.sum(e, axis=0, keepdims=True)

    sub = min(OUTPROJ_SUB, tm)
    blocks = [slice(r0, r0 + sub) for r0 in range(0, tm, sub)]
    pending = None
    for rows in blocks:
        acc = project(rows)
        if pending is not None:
            finish(*pending)
        pending = (rows, acc)
    finish(*pending)


def _outproj(ya, yg, o_f, o_b, rest, x, mod, dn_norm, w_out_bf, g_post1, g_pre2, w_router_t):
    b, n, d = x.shape
    tm = min(512, n)
    mod_map = (lambda bi, i: (bi, 0, 0)) if mod.shape[0] == b else (lambda bi, i: (0, 0, 0))
    tok = lambda w: pl.BlockSpec((1, tm, w), lambda bi, i: (bi, i, 0))
    vec = lambda w: pl.BlockSpec((1, w), lambda bi, i: (0, 0))
    return pl.pallas_call(
        functools.partial(_outproj_body, tm=tm),
        grid=(b, n // tm),
        in_specs=[tok(ATTN_W), tok(GM_W), tok(DN_W), tok(DN_W),
                  pl.BlockSpec((1, tm, DN_W), lambda bi, i: (bi, i, REST_DZ // DN_W)),
                  tok(d), pl.BlockSpec((1, 6, d), mod_map), vec(HEAD_DIM),
                  pl.BlockSpec((MIX_W, d), lambda bi, i: (0, 0), pipeline_mode=pl.Buffered(1)),
                  vec(d), vec(d), pl.BlockSpec((N_EXPERTS, d), lambda bi, i: (0, 0))],
        out_specs=[tok(d), tok(d), pl.BlockSpec((1, N_EXPERTS, tm), lambda bi, i: (bi, 0, i))],
        out_shape=[jax.ShapeDtypeStruct((b, n, d), F32), jax.ShapeDtypeStruct((b, n, d), F32),
                   jax.ShapeDtypeStruct((b, N_EXPERTS, n), F32)],
        compiler_params=_cparams(("parallel", "arbitrary"), V7X_VMEM_LIMIT),
        name="outproj_router",
    )(ya, yg, o_f, o_b, rest, x, mod, dn_norm.reshape(1, HEAD_DIM), w_out_bf, g_post1.reshape(1, d),
      g_pre2.reshape(1, d), w_router_t)


SUBLANES = 8
TOK_BITS = 12
ISSUE_GROUPS = 4


def _ffn_body(code_ref, h2_ref, gate_ref, w1_ref, w3_ref, w2_ref, ytm_ref, xs_scr, ys_scr, gsem, ssem,
              *, cap, n_exp, nb):
    e = pl.program_id(0)
    b = pl.program_id(1)
    t = e * nb + b
    slot = t & 1
    groups = cap // SUBLANES
    last = n_exp * nb - 1

    def gather(e2, b2, slot2):
        base = (b2 * n_exp + e2) * cap

        def body(gg, carry):
            for k in range(ISSUE_GROUPS):
                g = gg * ISSUE_GROUPS + k
                for j in range(SUBLANES):
                    tok = code_ref[base + g * SUBLANES + j] & ((1 << TOK_BITS) - 1)
                    pltpu.make_async_copy(h2_ref.at[b2, tok >> 3, pl.ds(tok & 7, 1)],
                                          xs_scr.at[slot2, g, pl.ds(j, 1)], gsem.at[slot2]).start()
            return carry
        lax.fori_loop(0, groups // ISSUE_GROUPS, body, 0)

    def scatter_wait(slot2):
        pltpu.make_async_copy(ys_scr.at[slot2], ytm_ref.at[pl.ds(0, groups)], ssem.at[slot2]).wait()

    @pl.when(t == 0)
    def _():
        gather(0, 0, 0)

    wrap = b + 1 == nb
    e_n = jnp.where(t < last, jnp.where(wrap, e + 1, e), e)
    b_n = jnp.where(t < last, jnp.where(wrap, 0, b + 1), b)
    base_n = (b_n * n_exp + e_n) * cap
    base = (b * n_exp + e) * cap
    row0 = b * n_exp * cap

    def gather_rows(lo_, hi_):
        for s in range(lo_, hi_):
            tok = code_ref[base_n + s] & ((1 << TOK_BITS) - 1)
            pltpu.make_async_copy(h2_ref.at[b_n, tok >> 3, pl.ds(tok & 7, 1)],
                                  xs_scr.at[1 - slot, s // SUBLANES, pl.ds(s % SUBLANES, 1)],
                                  gsem.at[1 - slot]).start()

    def scatter_rows(lo_, hi_):
        for s in range(lo_, hi_):
            p = row0 + (code_ref[base + s] >> TOK_BITS)
            pltpu.make_async_copy(ys_scr.at[slot, s // SUBLANES, pl.ds(s % SUBLANES, 1)],
                                  ytm_ref.at[p >> 3, pl.ds(p & 7, 1)], ssem.at[slot]).start()

    pltpu.make_async_copy(h2_ref.at[0, pl.ds(0, groups)], xs_scr.at[slot], gsem.at[slot]).wait()

    @pl.when(t >= 2)
    def _():
        scatter_wait(slot)

    gpr = min(groups, 256 // SUBLANES)
    rows = gpr * SUBLANES
    done = None
    for g0 in range(0, groups, gpr):
        r0 = g0 * SUBLANES
        xb = xs_scr[slot, g0:g0 + gpr].reshape(rows, -1).astype(BF16)
        a = _dot(xb, w1_ref[0])
        b3 = _dot(xb, w3_ref[0])
        gather_rows(r0, r0 + rows)
        act = (_silu(a) * b3).astype(BF16)
        y = _dot(act, w2_ref[0]) * gate_ref[0, 0, r0:r0 + rows, :]
        if done is not None:
            scatter_rows(*done)
        ys_scr[slot, g0:g0 + gpr] = y.reshape(gpr, SUBLANES, -1)
        done = (r0, r0 + rows)
    scatter_rows(*done)

    @pl.when(t == last)
    def _():
        pltpu.make_async_copy(h2_ref.at[0, pl.ds(0, groups)], xs_scr.at[1 - slot], gsem.at[1 - slot]).wait()
        if last >= 1:
            scatter_wait(1 - slot)
        scatter_wait(slot)


def _expert_ffn(code_flat, h2, gate, w1, w3, w2, layer):
    nb, n, d = h2.shape
    _, n_exp, _, ff = w1.shape
    cap = code_flat.shape[0] // (n_exp * nb)
    groups = cap // SUBLANES
    grid_spec = pltpu.PrefetchScalarGridSpec(
        num_scalar_prefetch=1,
        grid=(n_exp, nb),
        in_specs=[pl.BlockSpec(memory_space=pl.ANY),
                  pl.BlockSpec((1, 1, cap, 1), lambda e, b, code: (b, e, 0, 0)),
                  pl.BlockSpec((None, 1, d, ff), lambda e, b, code: (layer, e, 0, 0)),
                  pl.BlockSpec((None, 1, d, ff), lambda e, b, code: (layer, e, 0, 0)),
                  pl.BlockSpec((None, 1, ff, d), lambda e, b, code: (layer, e, 0, 0))],
        out_specs=pl.BlockSpec(memory_space=pl.ANY),
        scratch_shapes=[pltpu.VMEM((2, groups, SUBLANES, d), F32), pltpu.VMEM((2, groups, SUBLANES, d), F32),
                        pltpu.SemaphoreType.DMA((2,)), pltpu.SemaphoreType.DMA((2,))],
    )
    return pl.pallas_call(
        functools.partial(_ffn_body, cap=cap, n_exp=n_exp, nb=nb),
        grid_spec=grid_spec,
        out_shape=jax.ShapeDtypeStruct((nb * n_exp * groups, SUBLANES, d), F32),
        compiler_params=_cparams(("arbitrary", "arbitrary"), V7X_VMEM_LIMIT),
        name="expert_ffn",
    )(code_flat, h2.reshape(nb, n // SUBLANES, SUBLANES, d), gate, w1, w3, w2)


COMBINE_SLOTS = 4


def _combine_body(off_ref, ytm_ref, tok_ref, x1_ref, mod_ref, gpost_ref, o_ref, ybuf, sem, prog,
                  *, rows_pb, nt, nb):
    b = pl.program_id(0)
    tile = pl.program_id(1)
    d = o_ref.shape[-1]
    total = rows_pb // STEP

    def chunk_copy(c):
        slot = c & (COMBINE_SLOTS - 1)
        return pltpu.make_async_copy(ytm_ref.at[pl.ds(b * rows_pb + c * STEP, STEP)], ybuf.at[slot], sem.at[slot])

    @pl.when(tile == 0)
    def _():
        prog[0] = 0
        prog[1] = 0

    o0 = off_ref[b * (nt + 1) + tile]
    o1 = off_ref[b * (nt + 1) + tile + 1]
    c0 = o0 >> 7
    nchunks = jnp.where(o1 > o0, ((o1 + STEP - 1) >> 7) - c0, 0)

    o_ref[0] = jnp.zeros((STEP, d), F32)
    first = tok_ref[0, :, 0:1]
    end = first + tok_ref[0, :, 1:2]
    lane = lax.broadcasted_iota(I32, (STEP, STEP), 1)

    def body(i, carry):
        c = c0 + i
        issued = prog[0]
        target = jnp.minimum(c + COMBINE_SLOTS, total)

        def issue(k, carry2):
            chunk_copy(k).start()
            return carry2
        lax.fori_loop(issued, target, issue, 0)
        prog[0] = jnp.maximum(issued, target)

        @pl.when(prog[1] <= c)
        def _():
            chunk_copy(c).wait()
            prog[1] = c + 1

        posn = (c * STEP + lane).astype(F32)
        onehot = jnp.where((posn >= first) & (posn < end), 1.0, 0.0).astype(BF16)
        y = ybuf[c & (COMBINE_SLOTS - 1)]
        hi = y.astype(BF16)
        lo = (y - hi.astype(F32)).astype(BF16)
        o_ref[0] += _dot(onehot, hi) + _dot(onehot, lo)
        return carry
    lax.fori_loop(0, nchunks, body, 0)

    acc = o_ref[0]
    rr = acc * lax.rsqrt(jnp.mean(acc * acc, axis=-1, keepdims=True) + NORM_EPS) * gpost_ref[...]
    o_ref[0] = x1_ref[0] + mod_ref[0, 5:6, :] * rr


def _combine(off_flat, ytm, tokcol, x1, mod, g_post2):
    nb, n, d = x1.shape
    nt = n // STEP
    rows_pb = ytm.shape[0] // nb
    mod_map = (lambda b, i, *_: (b, 0, 0)) if mod.shape[0] == nb else (lambda b, i, *_: (0, 0, 0))
    grid_spec = pltpu.PrefetchScalarGridSpec(
        num_scalar_prefetch=1,
        grid=(nb, nt),
        in_specs=[pl.BlockSpec(memory_space=pl.ANY),
                  pl.BlockSpec((1, STEP, 128), lambda b, i, *_: (b, i, 0)),
                  pl.BlockSpec((1, STEP, d), lambda b, i, *_: (b, i, 0)),
                  pl.BlockSpec((1, 6, d), mod_map),
                  pl.BlockSpec((1, d), lambda b, i, *_: (0, 0))],
        out_specs=pl.BlockSpec((1, STEP, d), lambda b, i, *_: (b, i, 0)),
        scratch_shapes=[pltpu.VMEM((COMBINE_SLOTS, STEP, d), F32), pltpu.SemaphoreType.DMA((COMBINE_SLOTS,)),
                        pltpu.SMEM((2,), I32)],
    )
    return pl.pallas_call(
        functools.partial(_combine_body, rows_pb=rows_pb, nt=nt, nb=nb),
        grid_spec=grid_spec,
        out_shape=jax.ShapeDtypeStruct((nb, n, d), F32),
        compiler_params=_cparams(("arbitrary", "arbitrary"), 40 * 1024 * 1024),
        name="moe_combine",
    )(off_flat, ytm, tokcol, x1, mod, g_post2.reshape(1, d))


ROUTE_BISECTIONS = 24


def _route_body(aff_ref, code_ref, gate_ref, tokcol_ref, off_ref, slot_scr, sel_scr, ph_scr, pl_scr, gh_scr,
                gm_scr, gl_scr, *, n, cap):
    n_exp = N_EXPERTS
    nblk = n // STEP
    aff = aff_ref[0]
    ones_where = lambda m: jnp.where(m, 1.0, 0.0).astype(F32)

    thr = jnp.zeros((n_exp, 1), I32)
    for bit in range(30, -1, -1):
        cand = thr | (1 << bit)
        enough = jnp.sum(ones_where(aff >= pltpu.bitcast(cand, F32)), axis=1, keepdims=True) >= cap
        thr = jnp.where(enough, cand, thr)
    lo = pltpu.bitcast(thr, F32)
    hi = pltpu.bitcast(thr + 1, F32)
    for _ in range(ROUTE_BISECTIONS):
        mid = 0.5 * (lo + hi)
        enough = jnp.sum(ones_where(aff >= mid), axis=1, keepdims=True) >= cap
        lo = jnp.where(enough, mid, lo)
        hi = jnp.where(enough, hi, mid)
    gt = aff >= hi
    eq = (aff >= lo) & (aff < hi)
    need = cap - jnp.sum(ones_where(gt), axis=1, keepdims=True)
    tl = lax.broadcasted_iota(I32, (n_exp, n), 1)
    last = jnp.zeros((n_exp, 1), I32)
    for bit in range(n.bit_length() - 1, -1, -1):
        cand = last | (1 << bit)
        c = jnp.sum(ones_where(eq & (tl < cand)), axis=1, keepdims=True)
        last = jnp.where(c < need, cand, last)
    sel = ones_where(gt | (eq & (tl <= last) & (need > 0)))

    r16 = lax.broadcasted_iota(I32, (n_exp, n_exp), 0)
    c16 = lax.broadcasted_iota(I32, (n_exp, n_exp), 1)
    rank = _dot(jnp.where(c16 < r16, 1.0, 0.0).astype(BF16), sel.astype(BF16))
    cnt = jnp.sum(sel, axis=0, keepdims=True)

    rt = lax.broadcasted_iota(I32, (STEP, STEP), 0)
    ct = lax.broadcasted_iota(I32, (STEP, STEP), 1)
    upper = jnp.where(rt <= ct, 1.0, 0.0).astype(BF16)
    stacked = jnp.concatenate([sel, jnp.broadcast_to(cnt, (SUBLANES, n))], axis=0)
    base = jnp.zeros((n_exp + SUBLANES, 1), F32)
    lane = lax.broadcasted_iota(I32, (1, 128), 1)
    row8 = lax.broadcasted_iota(I32, (SUBLANES, STEP), 0)
    offs = jnp.zeros((1, 128), F32)
    for j in range(nblk):
        blk = slice(j * STEP, (j + 1) * STEP)
        incl = _dot(stacked[:, blk].astype(BF16), upper) + base
        offs = jnp.where(lane == j, base[n_exp:n_exp + 1, :], offs)
        base = incl[:, STEP - 1:STEP]
        cnt_b = cnt[:, blk]
        first_b = incl[n_exp:n_exp + 1, :] - cnt_b
        pos_b = first_b + rank[:, blk]
        pos_hi = jnp.floor(pos_b * (1.0 / STEP))
        slot_scr[:, blk] = incl[:n_exp, :] - 1.0
        sel_scr[:, blk] = sel[:, blk]
        ph_scr[:, blk] = pos_hi
        pl_scr[:, blk] = pos_b - pos_hi * STEP
        g = aff[:, blk]
        gh = g.astype(BF16).astype(F32)
        gm = (g - gh).astype(BF16).astype(F32)
        gh_scr[:, blk] = gh
        gm_scr[:, blk] = gm
        gl_scr[:, blk] = g - gh - gm
        cols = jnp.where(row8 == 0, first_b, jnp.where(row8 == 1, cnt_b, 0.0))
        tokcol_ref[0, blk, :] = jnp.concatenate([cols, jnp.zeros((STEP - SUBLANES, STEP), F32)], axis=0).T
    offs = jnp.where(lane == nblk, base[n_exp:n_exp + 1, :], offs)
    off_ref[0] = offs.astype(I32)

    s_iota = lax.broadcasted_iota(I32, (cap, STEP), 0).astype(F32)
    tok_lo = lax.broadcasted_iota(I32, (1, STEP), 1).astype(F32)

    code_ref[0] = jnp.zeros((n_exp, cap), I32)
    gate_ref[0] = jnp.zeros((n_exp, cap), F32)
    rowc = lax.broadcasted_iota(I32, (SUBLANES, cap), 0)

    def per_expert(e, carry):
        e8 = pl.multiple_of((e >> 3) << 3, SUBLANES)
        mine8 = row8 == (e & 7)
        acc = jnp.zeros((SUBLANES, cap), F32)
        for j in range(nblk):
            blk = slice(j * STEP, (j + 1) * STEP)
            row = lambda scr: jnp.sum(jnp.where(mine8, scr[pl.ds(e8, SUBLANES), blk], 0.0), axis=0, keepdims=True)
            onehot = jnp.where((row(slot_scr) == s_iota) & (row(sel_scr) > 0.0), 1.0, 0.0).astype(BF16)
            vals = jnp.where(row8 == 0, float(j), jnp.where(row8 == 1, tok_lo, jnp.where(
                row8 == 2, row(ph_scr), jnp.where(row8 == 3, row(pl_scr), jnp.where(
                    row8 == 4, row(gh_scr), jnp.where(row8 == 5, row(gm_scr), jnp.where(
                        row8 == 6, row(gl_scr), 0.0)))))))
            acc = acc + _dot_nt(vals.astype(BF16), onehot)
        ints = acc[0:4, :].astype(I32)
        token = ints[0:1] * STEP + ints[1:2]
        position = ints[2:3] * STEP + ints[3:4]
        minec = rowc == (e & 7)
        code_ref[0, pl.ds(e8, SUBLANES), :] = jnp.where(minec, (position << TOK_BITS) | token,
                                                        code_ref[0, pl.ds(e8, SUBLANES), :])
        gate_ref[0, pl.ds(e8, SUBLANES), :] = jnp.where(minec, (acc[4:5] + acc[5:6]) + acc[6:7],
                                                        gate_ref[0, pl.ds(e8, SUBLANES), :])
        return carry
    lax.fori_loop(0, n_exp, per_expert, 0)


def _route(aff_t, cap):
    b, n_exp, n = aff_t.shape
    assert n <= 1 << TOK_BITS and n // STEP < 128
    scr = pltpu.VMEM((n_exp, n), F32)
    return pl.pallas_call(
        functools.partial(_route_body, n=n, cap=cap),
        grid=(b,),
        in_specs=[pl.BlockSpec((1, n_exp, n), lambda bi: (bi, 0, 0))],
        out_specs=[pl.BlockSpec((1, n_exp, cap), lambda bi: (bi, 0, 0)),
                   pl.BlockSpec((1, n_exp, cap), lambda bi: (bi, 0, 0)),
                   pl.BlockSpec((1, n, 128), lambda bi: (bi, 0, 0)),
                   pl.BlockSpec((1, 1, 128), lambda bi: (bi, 0, 0))],
        out_shape=[jax.ShapeDtypeStruct((b, n_exp, cap), I32), jax.ShapeDtypeStruct((b, n_exp, cap), F32),
                   jax.ShapeDtypeStruct((b, n, 128), F32), jax.ShapeDtypeStruct((b, 1, 128), I32)],
        scratch_shapes=[scr] * 7,
        compiler_params=_cparams(("arbitrary",), 40 * 1024 * 1024),
        name="moe_route",
    )(aff_t)


def _moe(aff_t, h2, x1, mod, g_post2, w1, w3, w2, layer):
    b, n, d = x1.shape
    cap = EC_CAPACITY * n // N_EXPERTS
    code, gate, tokcol, off = _route(aff_t, cap)
    ytm = _expert_ffn(code.reshape(-1), h2, gate[..., None], w1, w3, w2, layer)
    rows = b * N_EXPERTS * cap
    return _combine(off[:, 0, :n // STEP + 1].reshape(-1), ytm.reshape(rows, d), tokcol, x1, mod, g_post2)


def _rope_tables(n):
    rows = n // GRID_W
    r = jnp.repeat(jnp.arange(rows, dtype=F32), GRID_W)
    col = jnp.tile(jnp.arange(GRID_W, dtype=F32), rows)
    half = HEAD_DIM // 2
    inv = ROPE_BASE ** (-jnp.arange(0, half, 2, dtype=F32) / half)
    ar, ac = r[:, None] * inv, col[:, None] * inv
    cr, sr, cc, sc = jnp.cos(ar), jnp.sin(ar), jnp.cos(ac), jnp.sin(ac)
    return (jnp.concatenate([cr, cr, cc, cc], axis=-1), jnp.concatenate([-sr, sr, -sc, sc], axis=-1))


def _prep_w_in(w_in):
    d = w_in.shape[0]
    main = w_in[:, :QKV_W + REST_GATE]
    gcols = w_in[:, QKV_W + REST_GATE:].reshape(d, 2, 2, DN_HEADS).transpose(0, 2, 1, 3).reshape(d, 4 * DN_HEADS)
    pad = jnp.zeros((d, IN_PAD_W - QKV_W - REST_GATE - 4 * DN_HEADS), w_in.dtype)
    return jnp.concatenate([main, gcols, pad], axis=1).astype(BF16)


def _lane_row(v8):
    return jnp.concatenate([v8.reshape(-1).astype(F32), jnp.zeros((128 - v8.size,), F32)]).reshape(1, 128)


def kernel(x, c, ctx, c_ctx, g_pre1, g_post1, g_pre2, g_post2, w_mod, b_mod, w_in, attn_sink, gm_ln, gm_ws, gm_bs,
           dn_conv, dn_a_log, dn_dt_bias, dn_norm, w_out, w_router, w_e1, w_e3, w_e2):
    b, n, d = x.shape
    depth = w_in.shape[0]
    rope_cs = _rope_tables(n)
    c8 = jnp.concatenate([c, c_ctx[None, :], jnp.zeros((8 - b - 1, d), F32)], axis=0)
    w1, w3, w2 = w_e1.astype(BF16), w_e3.astype(BF16), w_e2.astype(BF16)
    xc = ctx
    for l in range(depth):
        update_ctx = l < depth - 1
        mod_all = _modulation(c8, w_mod, b_mod[l], l)
        mod_lat = mod_all[:b].reshape(b, 6, d)
        mod_ctx = mod_all[b:b + 1].reshape(1, 6, d)
        w_in_bf = _prep_w_in(w_in[l])
        qkv, rest = _inproj(x, mod_lat, g_pre1[l], w_in_bf, rope_cs)
        qkvc, restc = _inproj(xc, mod_ctx, g_pre1[l], w_in_bf, None)
        ya = _attention(qkv, qkvc, attn_sink[l])
        ws_bf = gm_ws[l].astype(BF16)
        yg = _gmlp(rest, gm_ln[l], ws_bf, gm_bs[l])
        (o_f, o_b), (oc_f, oc_b) = _deltanet(rest, restc, _lane_row(dn_a_log[l]), _lane_row(dn_dt_bias[l]),
                                             dn_conv[l])
        w_out_bf = w_out[l].astype(BF16)
        w_router_t = w_router[l].T
        x1, h2, aff_t = _outproj(ya, yg, o_f, o_b, rest, x, mod_lat, dn_norm[l], w_out_bf, g_post1[l], g_pre2[l],
                                 w_router_t)
        x = _moe(aff_t, h2, x1, mod_lat, g_post2[l], w1, w3, w2, l)
        if update_ctx:
            yac = _ctx_attention(qkvc, attn_sink[l])
            ygc = _gmlp(restc, gm_ln[l], ws_bf, gm_bs[l])
            xc1, h2c, affc_t = _outproj(yac, ygc, oc_f, oc_b, restc, xc, mod_ctx, dn_norm[l], w_out_bf, g_post1[l],
                                        g_pre2[l], w_router_t)
            xc = _moe(affc_t, h2c, xc1, mod_ctx, g_post2[l], w1, w3, w2, l)
    return x
```

```python
import functools

import jax
import jax.numpy as jnp
from jax import lax
from jax.experimental import pallas as pl
from jax.experimental.pallas import tpu as pltpu

F32 = jnp.float32
BF16 = jnp.bfloat16
I32 = jnp.int32

HEAD_DIM = 128
GRID_W = 64
ATTN_HEADS = 8
ATTN_KV_HEADS = 2
ATTN_GROUP = ATTN_HEADS // ATTN_KV_HEADS
WINDOW = 128
ATTN_BLOCK = 128
ROPE_BASE = 10000.0
MASK_VALUE = -1e30
GM_HEADS = 4
GM_CHUNK = 128
DN_HEADS = 4
DN_CHUNK = 64
N_EXPERTS = 16
EC_CAPACITY = 2
NORM_EPS = 1e-6

ATTN_W = ATTN_HEADS * HEAD_DIM
KV_W = ATTN_KV_HEADS * HEAD_DIM
QKV_W = ATTN_W + 2 * KV_W
GM_W = GM_HEADS * HEAD_DIM
DN_W = DN_HEADS * HEAD_DIM
MIX_W = ATTN_W + GM_W + DN_W
REST_GU, REST_GV, REST_DQ, REST_DK, REST_DV, REST_DZ, REST_GATE = 0, 512, 1024, 1536, 2048, 2560, 3072
REST_W = 3200
IN_PAD_W = QKV_W + REST_W

V7X_VMEM_LIMIT = 56 * 1024 * 1024
STEP = 128


def _cparams(sem, vmem=None):
    return pltpu.CompilerParams(dimension_semantics=sem, vmem_limit_bytes=vmem)


def _silu(x):
    return x * jax.nn.sigmoid(x)


def _dot(a, b):
    return jnp.dot(a, b, preferred_element_type=F32)


def _dot_nt(a, b):
    return lax.dot_general(a, b, (((1,), (1,)), ((), ())), preferred_element_type=F32)


def _split2(a):
    hi = a.astype(BF16)
    lo = (a - hi.astype(F32)).astype(BF16)
    return hi, lo


def _split3(a):
    hi = a.astype(BF16)
    r = a - hi.astype(F32)
    mid = r.astype(BF16)
    lo = (r - mid.astype(F32)).astype(BF16)
    return hi, mid, lo


def _dot_x3(a, b, nt=False):
    d = _dot_nt if nt else _dot
    ah, al = _split2(a)
    bh, bl = _split2(b)
    return d(ah, bh) + d(ah, bl) + d(al, bh)


def _mod_body(c_ref, w_ref, b_ref, o_ref):
    s = _silu(c_ref[...]).astype(BF16)
    o_ref[...] = _dot(s, w_ref[0].astype(BF16)) + b_ref[...]


def _modulation(c8, w_mod_all, b_mod, layer):
    _, d, n6 = w_mod_all.shape
    tn = 1024
    return pl.pallas_call(
        _mod_body,
        grid=(n6 // tn,),
        in_specs=[pl.BlockSpec((8, d), lambda j: (0, 0)),
                  pl.BlockSpec((1, d, tn), lambda j: (layer, 0, j)),
                  pl.BlockSpec((1, tn), lambda j: (0, j))],
        out_specs=pl.BlockSpec((8, tn), lambda j: (0, j)),
        out_shape=jax.ShapeDtypeStruct((8, n6), F32),
        compiler_params=_cparams(("arbitrary",), 40 * 1024 * 1024),
        name="modulation",
    )(c8, w_mod_all, b_mod.reshape(1, n6))


def _inproj_body(*refs, rope, tm):
    if rope:
        x_ref, mod_ref, g_ref, w_ref, cos_ref, sin_ref, qkv_ref, rest_ref = refs
    else:
        x_ref, mod_ref, g_ref, w_ref, qkv_ref, rest_ref = refs
    x = x_ref[0]
    ms = jnp.mean(x * x, axis=-1, keepdims=True)
    y = x * lax.rsqrt(ms + NORM_EPS) * g_ref[...]
    h = y * (1.0 + mod_ref[0, 1:2, :]) + mod_ref[0, 0:1, :]
    hb = h.astype(BF16)
    if rope:
        cosf = cos_ref[...]
        sinf = sin_ref[...]
        lane = lax.broadcasted_iota(I32, (tm, HEAD_DIM), 1)
        first = (lane & 32) == 0

    def rot(t):
        if not rope:
            return t
        partner = jnp.where(first, pltpu.roll(t, 96, 1), pltpu.roll(t, 32, 1))
        return t * cosf + partner * sinf

    for c0 in (0, 512):
        t = _dot(hb, w_ref[:, c0:c0 + 512])
        for j in range(4):
            th = rot(t[:, j * 128:(j + 1) * 128]) * (HEAD_DIM ** -0.5)
            qkv_ref[0, :, c0 + j * 128:c0 + (j + 1) * 128] = th.astype(BF16)
    t = _dot(hb, w_ref[:, ATTN_W:ATTN_W + 512])
    for j in range(2):
        qkv_ref[0, :, ATTN_W + j * 128:ATTN_W + (j + 1) * 128] = rot(t[:, j * 128:(j + 1) * 128]).astype(BF16)
    qkv_ref[0, :, ATTN_W + KV_W:QKV_W] = t[:, 256:512].astype(BF16)
    for c0 in range(0, REST_W, 512):
        cw = min(512, REST_W - c0)
        rest_ref[0, :, c0:c0 + cw] = _dot(hb, w_ref[:, QKV_W + c0:QKV_W + c0 + cw])


def _inproj(x, mod, g, w_bf, rope_cs):
    b, t, d = x.shape
    tm = 256
    rope = rope_cs is not None
    mod_map = (lambda bi, i: (bi, 0, 0)) if mod.shape[0] == b else (lambda bi, i: (0, 0, 0))
    in_specs = [pl.BlockSpec((1, tm, d), lambda bi, i: (bi, i, 0)),
                pl.BlockSpec((1, 6, d), mod_map),
                pl.BlockSpec((1, d), lambda bi, i: (0, 0)),
                pl.BlockSpec((d, IN_PAD_W), lambda bi, i: (0, 0), pipeline_mode=pl.Buffered(1))]
    args = [x, mod, g.reshape(1, d), w_bf]
    if rope:
        in_specs += [pl.BlockSpec((tm, HEAD_DIM), lambda bi, i: (i, 0)),
                     pl.BlockSpec((tm, HEAD_DIM), lambda bi, i: (i, 0))]
        args += list(rope_cs)
    return pl.pallas_call(
        functools.partial(_inproj_body, rope=rope, tm=tm),
        grid=(b, t // tm),
        in_specs=in_specs,
        out_specs=[pl.BlockSpec((1, tm, QKV_W), lambda bi, i: (bi, i, 0)),
                   pl.BlockSpec((1, tm, REST_W), lambda bi, i: (bi, i, 0))],
        out_shape=[jax.ShapeDtypeStruct((b, t, QKV_W), BF16),
                   jax.ShapeDtypeStruct((b, t, REST_W), F32)],
        compiler_params=_cparams(("parallel", "arbitrary"), V7X_VMEM_LIMIT),
        name="inproj_rope" if rope else "inproj",
    )(*args)


def _group_queries(q_ref, rows, kh):
    return jnp.concatenate(
        [q_ref[0, rows, (kh * ATTN_GROUP + g) * HEAD_DIM:(kh * ATTN_GROUP + g + 1) * HEAD_DIM]
         for g in range(ATTN_GROUP)], axis=0)


def _softmax_pv_blocks(sink_ref, o_ref, blocks, scores, values):
    probs, dens = {}, {}
    for blk in blocks:
        _, kh = blk
        sink_col = jnp.concatenate(
            [jnp.full((ATTN_BLOCK, 1), sink_ref[kh * ATTN_GROUP + g], F32) for g in range(ATTN_GROUP)], axis=0)
        mx = sink_col
        for s in scores[blk]:
            mx = jnp.maximum(mx, jnp.max(s, axis=-1, keepdims=True))
        probs[blk] = [jnp.exp(s - mx).astype(BF16) for s in scores[blk]]
        dens[blk] = jnp.exp(sink_col - mx)
    for blk in blocks:
        j, kh = blk
        acc = None
        for e, v in zip(probs[blk], values[blk]):
            pv = _dot(e, jnp.concatenate([v, jnp.ones_like(v)], axis=1))
            acc = pv if acc is None else acc + pv
        o = acc[:, :HEAD_DIM] / (dens[blk] + acc[:, HEAD_DIM:HEAD_DIM + 1])
        rows = slice(j * ATTN_BLOCK, (j + 1) * ATTN_BLOCK)
        for g in range(ATTN_GROUP):
            hh = kh * ATTN_GROUP + g
            o_ref[0, rows, hh * HEAD_DIM:(hh + 1) * HEAD_DIM] = o[g * ATTN_BLOCK:(g + 1) * ATTN_BLOCK].astype(BF16)


def _attn_body(sink_ref, q_ref, km_ref, vm_ref, kp_ref, vp_ref, kn_ref, vn_ref, kc_ref, vc_ref, o_ref, *, tq, n):
    i = pl.program_id(1)
    qb = tq // ATTN_BLOCK
    qi = lax.broadcasted_iota(I32, (ATTN_BLOCK, 3 * ATTN_BLOCK), 0)
    sj = lax.broadcasted_iota(I32, (ATTN_BLOCK, 3 * ATTN_BLOCK), 1) - ATTN_BLOCK

    def band(main_ref, prev_ref, next_ref, j, hs):
        pieces = []
        for blk in (j - 1, j, j + 1):
            if blk < 0:
                pieces.append(prev_ref[0, :, hs])
            elif blk >= qb:
                pieces.append(next_ref[0, :, hs])
            else:
                pieces.append(main_ref[0, blk * ATTN_BLOCK:(blk + 1) * ATTN_BLOCK, hs])
        return jnp.concatenate(pieces, axis=0)

    blocks = [(j, kh) for j in range(qb) for kh in range(ATTN_KV_HEADS)]
    scores, values = {}, {}
    for j in range(qb):
        kpos = (i * qb + j) * ATTN_BLOCK + sj
        ok = (jnp.abs(sj - qi) <= WINDOW) & (kpos >= 0) & (kpos < n)
        bias = jnp.where(ok, 0.0, MASK_VALUE).astype(F32)
        bias4 = jnp.concatenate([bias] * ATTN_GROUP, axis=0)
        rows = slice(j * ATTN_BLOCK, (j + 1) * ATTN_BLOCK)
        for kh in range(ATTN_KV_HEADS):
            hs = slice(kh * HEAD_DIM, (kh + 1) * HEAD_DIM)
            q = _group_queries(q_ref, rows, kh)
            scores[(j, kh)] = [_dot_nt(q, band(km_ref, kp_ref, kn_ref, j, hs)) + bias4,
                               _dot_nt(q, kc_ref[0, :, hs])]
            values[(j, kh)] = [band(vm_ref, vp_ref, vn_ref, j, hs), vc_ref[0, :, hs]]
    _softmax_pv_blocks(sink_ref, o_ref, blocks, scores, values)


def _attention(qkv, qkvc, sink):
    b, n, _ = qkv.shape
    m = qkvc.shape[1]
    tq = 512
    qb = tq // ATTN_BLOCK
    nb = n // ATTN_BLOCK
    kcol, vcol = ATTN_W // KV_W, ATTN_W // KV_W + 1
    smem = pl.BlockSpec(memory_space=pltpu.SMEM)
    main = lambda col: pl.BlockSpec((1, tq, KV_W), lambda bi, i: (bi, i, col))
    prev = lambda col: pl.BlockSpec((1, ATTN_BLOCK, KV_W), lambda bi, i: (bi, jnp.maximum(i * qb - 1, 0), col))
    nxt = lambda col: pl.BlockSpec((1, ATTN_BLOCK, KV_W), lambda bi, i: (bi, jnp.minimum(i * qb + qb, nb - 1), col))
    ctx = lambda col: pl.BlockSpec((1, m, KV_W), lambda bi, i: (bi, 0, col))
    return pl.pallas_call(
        functools.partial(_attn_body, tq=tq, n=n),
        grid=(b, n // tq),
        in_specs=[smem, pl.BlockSpec((1, tq, ATTN_W), lambda bi, i: (bi, i, 0)),
                  main(kcol), main(vcol), prev(kcol), prev(vcol), nxt(kcol), nxt(vcol), ctx(kcol), ctx(vcol)],
        out_specs=pl.BlockSpec((1, tq, ATTN_W), lambda bi, i: (bi, i, 0)),
        out_shape=jax.ShapeDtypeStruct((b, n, ATTN_W), BF16),
        compiler_params=_cparams(("parallel", "arbitrary"), 40 * 1024 * 1024),
        name="window_attention",
    )(sink, qkv, qkv, qkv, qkv, qkv, qkv, qkv, qkvc, qkvc)


def _ctx_attn_body(sink_ref, q_ref, kc_ref, vc_ref, o_ref, *, m):
    blocks = [(j, kh) for j in range(m // ATTN_BLOCK) for kh in range(ATTN_KV_HEADS)]
    scores, values = {}, {}
    for j, kh in blocks:
        rows = slice(j * ATTN_BLOCK, (j + 1) * ATTN_BLOCK)
        hs = slice(kh * HEAD_DIM, (kh + 1) * HEAD_DIM)
        scores[(j, kh)] = [_dot_nt(_group_queries(q_ref, rows, kh), kc_ref[0, :, hs])]
        values[(j, kh)] = [vc_ref[0, :, hs]]
    _softmax_pv_blocks(sink_ref, o_ref, blocks, scores, values)


def _ctx_attention(qkvc, sink):
    b, m, _ = qkvc.shape
    kcol, vcol = ATTN_W // KV_W, ATTN_W // KV_W + 1
    return pl.pallas_call(
        functools.partial(_ctx_attn_body, m=m),
        grid=(b,),
        in_specs=[pl.BlockSpec(memory_space=pltpu.SMEM),
                  pl.BlockSpec((1, m, ATTN_W), lambda bi: (bi, 0, 0)),
                  pl.BlockSpec((1, m, KV_W), lambda bi: (bi, 0, kcol)),
                  pl.BlockSpec((1, m, KV_W), lambda bi: (bi, 0, vcol))],
        out_specs=pl.BlockSpec((1, m, ATTN_W), lambda bi: (bi, 0, 0)),
        out_shape=jax.ShapeDtypeStruct((b, m, ATTN_W), BF16),
        compiler_params=_cparams(("parallel",), 40 * 1024 * 1024),
        name="context_attention",
    )(sink, qkvc, qkvc, qkvc)


def _gmlp_body(u_ref, v_ref, ln_ref, ws_ref, bst_ref, o_ref, *, tg):
    for ci in range(tg // GM_CHUNK):
        rows = slice(ci * GM_CHUNK, (ci + 1) * GM_CHUNK)
        u = jax.nn.gelu(u_ref[0, rows, :])
        v = jax.nn.gelu(v_ref[0, rows, :])
        for h in range(GM_HEADS):
            hs = slice(h * HEAD_DIM, (h + 1) * HEAD_DIM)
            vh = v[:, hs]
            vh = vh - jnp.mean(vh, axis=-1, keepdims=True)
            vh = vh * lax.rsqrt(jnp.mean(vh * vh, axis=-1, keepdims=True) + NORM_EPS) * ln_ref[:, hs]
            mixed = _dot(ws_ref[h], vh.astype(BF16)) + bst_ref[:, h:h + 1]
            o_ref[0, rows, hs] = (u[:, hs] * mixed).astype(BF16)


def _gmlp(rest, gm_ln, gm_ws_bf, gm_bs):
    b, n, _ = rest.shape
    tg = min(512, n)
    return pl.pallas_call(
        functools.partial(_gmlp_body, tg=tg),
        grid=(b, n // tg),
        in_specs=[pl.BlockSpec((1, tg, GM_W), lambda bi, i: (bi, i, REST_GU // GM_W)),
                  pl.BlockSpec((1, tg, GM_W), lambda bi, i: (bi, i, REST_GV // GM_W)),
                  pl.BlockSpec((1, GM_W), lambda bi, i: (0, 0)),
                  pl.BlockSpec((GM_HEADS, GM_CHUNK, GM_CHUNK), lambda bi, i: (0, 0, 0)),
                  pl.BlockSpec((GM_CHUNK, GM_HEADS), lambda bi, i: (0, 0))],
        out_specs=pl.BlockSpec((1, tg, GM_W), lambda bi, i: (bi, i, 0)),
        out_shape=jax.ShapeDtypeStruct((b, n, GM_W), BF16),
        compiler_params=_cparams(("parallel", "arbitrary"), 40 * 1024 * 1024),
        name="gmlp",
    )(rest, rest, gm_ln.reshape(1, GM_W), gm_ws_bf, gm_bs.T)


def _dn_pre_body(alog_ref, dtb_ref, cw_ref, q_ref, k_ref, v_ref, qp_ref, kp_ref, vp_ref,
                 qn_ref, kn_ref, vn_ref, gate_ref, oq_ref, ok_ref, ov_ref, g_ref, gt_ref, *, td, nt):
    i = pl.program_id(1)
    row = lax.broadcasted_iota(I32, (td, DN_W), 0)
    has_prev = (i > 0).astype(F32)
    has_next = (i < nt - 1).astype(F32)

    def conv(x_ref, p_ref, n_ref, c0):
        x = x_ref[0]
        prev_row = p_ref[0, 7:8, :] * has_prev
        next_row = n_ref[0, 0:1, :] * has_next
        xm = jnp.where(row == 0, prev_row, pltpu.roll(x, 1, 0))
        xp = jnp.where(row == td - 1, next_row, pltpu.roll(x, td - 1, 0))
        y = (cw_ref[0:1, c0:c0 + DN_W] * xm + cw_ref[1:2, c0:c0 + DN_W] * x
             + cw_ref[2:3, c0:c0 + DN_W] * xp)
        return _silu(y)

    def l2n(y, h):
        yh = y[:, h * HEAD_DIM:(h + 1) * HEAD_DIM]
        return yh * lax.rsqrt(jnp.sum(yh * yh, axis=-1, keepdims=True) + NORM_EPS)

    yq = conv(q_ref, qp_ref, qn_ref, 0)
    yk = conv(k_ref, kp_ref, kn_ref, DN_W)
    for h in range(DN_HEADS):
        hs = slice(h * HEAD_DIM, (h + 1) * HEAD_DIM)
        oq_ref[0, :, hs] = l2n(yq, h) * (HEAD_DIM ** -0.5)
        ok_ref[0, :, hs] = l2n(yk, h)
    ov_ref[0] = conv(v_ref, vp_ref, vn_ref, 2 * DN_W)

    raw = gate_ref[0]
    z = raw + dtb_ref[...]
    softplus = jnp.maximum(z, 0.0) + jnp.log(1.0 + jnp.exp(-jnp.abs(z)))
    gval = -jnp.exp(alog_ref[...]) * softplus
    lane = lax.broadcasted_iota(I32, (td, 128), 1)
    gates = jnp.where(lane < 2 * DN_HEADS, gval, jax.nn.sigmoid(raw))
    g_ref[0] = gates
    gt_ref[0] = gates.T[0:16, :]


def _dn_pre(rest, alog_row, dtb_row, conv_w):
    b, n, _ = rest.shape
    td = 256
    nt = n // td
    cur = lambda col: pl.BlockSpec((1, td, DN_W), lambda bi, i: (bi, i, col))
    prev = lambda col: pl.BlockSpec((1, 8, DN_W), lambda bi, i: (bi, jnp.maximum(i * (td // 8) - 1, 0), col))
    nxt = lambda col: pl.BlockSpec((1, 8, DN_W), lambda bi, i: (bi, jnp.minimum((i + 1) * (td // 8), n // 8 - 1), col))
    cq, ck, cv = REST_DQ // DN_W, REST_DK // DN_W, REST_DV // DN_W
    row128 = pl.BlockSpec((1, 128), lambda bi, i: (0, 0))
    tok = lambda w: pl.BlockSpec((1, td, w), lambda bi, i: (bi, i, 0))
    return pl.pallas_call(
        functools.partial(_dn_pre_body, td=td, nt=nt),
        grid=(b, nt),
        in_specs=[row128, row128, pl.BlockSpec((3, 3 * DN_W), lambda bi, i: (0, 0)),
                  cur(cq), cur(ck), cur(cv), prev(cq), prev(ck), prev(cv), nxt(cq), nxt(ck), nxt(cv),
                  pl.BlockSpec((1, td, 128), lambda bi, i: (bi, i, REST_GATE // 128))],
        out_specs=[tok(DN_W), tok(DN_W), tok(DN_W), tok(128),
                   pl.BlockSpec((1, 16, td), lambda bi, i: (bi, 0, i))],
        out_shape=[jax.ShapeDtypeStruct((b, n, DN_W), F32)] * 3
        + [jax.ShapeDtypeStruct((b, n, 128), F32), jax.ShapeDtypeStruct((b, 16, n), F32)],
        compiler_params=_cparams(("parallel", "arbitrary"), 40 * 1024 * 1024),
        name="deltanet_pre",
    )(alog_row, dtb_row, conv_w, rest, rest, rest, rest, rest, rest, rest, rest, rest, rest)


def _dn_chunk_body(q_ref, k_ref, v_ref, g_ref, gt_ref, *out_refs, nb):
    dir_refs = (out_refs[:6], out_refs[6:])
    r = lax.broadcasted_iota(I32, (STEP, STEP), 0)
    c = lax.broadcasted_iota(I32, (STEP, STEP), 1)
    same = (r // DN_CHUNK) == (c // DN_CHUNK)
    incl = (same & (c <= r), same & (c >= r))
    strict = (same & (c < r), same & (c > r))
    lower_b = jnp.where(incl[0], 1.0, 0.0).astype(BF16)
    upper_b = jnp.where(incl[1], 1.0, 0.0).astype(BF16)
    same_b = jnp.where(same, 1.0, 0.0).astype(BF16)
    lane = lax.broadcasted_iota(I32, (STEP, 128), 1)

    gates, gc_col, gc_row, tot_col, tot_row = [], [], [], [], []
    for bi in range(nb):
        g = g_ref[bi]
        gt = gt_ref[bi]
        gh, gm, gl = _split3(g)
        th, tm_, tl = _split3(gt)
        cf_col = _dot(lower_b, gh) + _dot(lower_b, gm) + _dot(lower_b, gl)
        tc = _dot(same_b, gh) + _dot(same_b, gm) + _dot(same_b, gl)
        cf_row = _dot(th, upper_b) + _dot(tm_, upper_b) + _dot(tl, upper_b)
        tr = _dot(th, same_b) + _dot(tm_, same_b) + _dot(tl, same_b)
        gates.append(g)
        gc_col.append((cf_col, tc - cf_col + g))
        gc_row.append((cf_row, tr - cf_row + gt))
        tot_col.append(tc)
        tot_row.append(tr)

    heads = [(bi, h) for bi in range(nb) for h in range(DN_HEADS)]
    hsl = lambda h: slice(h * HEAD_DIM, (h + 1) * HEAD_DIM)
    kbf = {bh: k_ref[bh[0], :, hsl(bh[1])].astype(BF16) for bh in heads}
    kk = {bh: _dot_nt(kbf[bh], kbf[bh]) for bh in heads}
    qk = {bh: _dot_nt(q_ref[bh[0], :, hsl(bh[1])].astype(BF16), kbf[bh]) for bh in heads}

    probs = [(bi, h, d) for bi, h in heads for d in range(2)]
    dec, y, p = {}, {}, {}
    for pr in probs:
        bi, h, d = pr
        lg = d * DN_HEADS + h
        lb = 2 * DN_HEADS + d * DN_HEADS + h
        gcc = gc_col[bi][d][:, lg:lg + 1]
        gcr = gc_row[bi][d][lg:lg + 1, :]
        dec[pr] = jnp.exp(jnp.where(incl[d], gcc - gcr, MASK_VALUE))
        y[pr] = -jnp.where(strict[d], kk[(bi, h)] * gates[bi][:, lb:lb + 1] * dec[pr], 0.0)
        p[pr] = y[pr]
    for _ in range(5):
        for pr in probs:
            pb = p[pr].astype(BF16)
            p[pr] = _dot(pb, pb)
        for pr in probs:
            y[pr] = y[pr] + p[pr] + _dot(y[pr].astype(BF16), p[pr].astype(BF16))

    for pr in probs:
        bi, h, d = pr
        hs = hsl(h)
        u_ref, w_ref, qg_ref, kgt_ref, intra_ref, _ = dir_refs[d]
        lg = d * DN_HEADS + h
        lb = 2 * DN_HEADS + d * DN_HEADS + h
        gcc = gc_col[bi][d][:, lg:lg + 1]
        gcr = gc_row[bi][d][lg:lg + 1, :]
        totr = tot_row[bi][lg:lg + 1, :]
        beta = gates[bi][:, lb:lb + 1]
        q = q_ref[bi, :, hs]
        k = k_ref[bi, :, hs]
        egc = jnp.exp(gcc)
        rhs = jnp.concatenate([v_ref[bi, :, hs] * beta, k * (beta * egc)], axis=1)
        sol = rhs + _dot(y[pr].astype(BF16), rhs.astype(BF16))
        u_ref[bi, :, hs] = sol[:, :HEAD_DIM]
        w_ref[bi, :, hs] = sol[:, HEAD_DIM:].astype(BF16)
        intra_ref[bi, :, hs] = (qk[(bi, h)] * dec[pr]).astype(BF16)
        qg_ref[bi, :, hs] = (q * egc).astype(BF16)
        kgt_ref[bi, hs, :] = (k.T * jnp.exp(totr - gcr)).astype(BF16)

    for bi in range(nb):
        for d in range(2):
            eg = jnp.zeros((STEP, 128), F32)
            for h in range(DN_HEADS):
                lg = d * DN_HEADS + h
                eg = jnp.where(lane == h, jnp.exp(tot_col[bi][:, lg:lg + 1]), eg)
            dir_refs[d][5][bi] = eg


DN_CHUNK_NB = 2


def _dn_chunk(dq, dk, dv, gates, gates_t):
    b, n, _ = dq.shape
    nb = DN_CHUNK_NB if b % DN_CHUNK_NB == 0 else 1
    tok = lambda w: pl.BlockSpec((nb, STEP, w), lambda bi, i: (bi, i, 0))
    one_dir_specs = [tok(DN_W), tok(DN_W), tok(DN_W),
                     pl.BlockSpec((nb, DN_W, STEP), lambda bi, i: (bi, 0, i)), tok(DN_W), tok(128)]
    one_dir_shapes = [jax.ShapeDtypeStruct((b, n, DN_W), F32), jax.ShapeDtypeStruct((b, n, DN_W), BF16),
                      jax.ShapeDtypeStruct((b, n, DN_W), BF16), jax.ShapeDtypeStruct((b, DN_W, n), BF16),
                      jax.ShapeDtypeStruct((b, n, DN_W), BF16), jax.ShapeDtypeStruct((b, n, 128), F32)]
    outs = pl.pallas_call(
        functools.partial(_dn_chunk_body, nb=nb),
        grid=(b // nb, n // STEP),
        in_specs=[tok(DN_W), tok(DN_W), tok(DN_W), tok(128),
                  pl.BlockSpec((nb, 16, STEP), lambda bi, i: (bi, 0, i))],
        out_specs=one_dir_specs * 2,
        out_shape=one_dir_shapes * 2,
        compiler_params=_cparams(("parallel", "arbitrary"), 48 * 1024 * 1024),
        name="deltanet_chunk",
    )(dq, dk, dv, gates, gates_t)
    return outs[:6], outs[6:]


def _dn_scan_body(*refs, nb, ns):
    dir_in = (refs[0:7], refs[7:14])
    o_refs = refs[14:16]
    sout_refs = refs[16:18]
    s_scr = refs[18]
    i = pl.program_id(0)

    @pl.when(i == 0)
    def _():
        s_scr[0] = dir_in[0][6][...]
        s_scr[1] = dir_in[1][6][...]

    zeros = jnp.zeros((DN_CHUNK, HEAD_DIM), F32)
    chains = [(d, bi, h) for d in range(2) for bi in range(nb) for h in range(DN_HEADS)]
    state = {ch: s_scr[ch[0], ch[1], ch[2]] for ch in chains}
    for pos in range(2):
        ws = {}
        for ch in chains:
            d, bi, h = ch
            ci = pos if d == 0 else 1 - pos
            rows = slice(ci * DN_CHUNK, (ci + 1) * DN_CHUNK)
            hs = slice(h * HEAD_DIM, (h + 1) * HEAD_DIM)
            wq = jnp.concatenate([dir_in[d][1][bi, rows, hs], dir_in[d][2][bi, rows, hs]], axis=0)
            ws[ch] = _dot(wq, state[ch].astype(BF16))
        for ch in chains:
            d, bi, h = ch
            u_ref, _, _, kgt_ref, intra_ref, eg_ref, _ = dir_in[d]
            ci = pos if d == 0 else 1 - pos
            rows = slice(ci * DN_CHUNK, (ci + 1) * DN_CHUNK)
            hs = slice(h * HEAD_DIM, (h + 1) * HEAD_DIM)
            v_new = u_ref[bi, rows, hs] - ws[ch][:DN_CHUNK]
            pieces = [v_new, zeros] if ci == 0 else [zeros, v_new]
            v_full = jnp.concatenate(pieces, axis=0).astype(BF16)
            o_refs[d][bi, rows, hs] = ws[ch][DN_CHUNK:] + _dot(intra_ref[bi, rows, hs], v_full)
            e = eg_ref[bi, rows, h:h + 1]
            state[ch] = state[ch] * jnp.concatenate([e, e], axis=0) + _dot(kgt_ref[bi, hs, :], v_full)
    for ch in chains:
        s_scr[ch[0], ch[1], ch[2]] = state[ch]

    @pl.when(i == ns - 1)
    def _():
        sout_refs[0][...] = s_scr[0]
        sout_refs[1][...] = s_scr[1]


def _dn_scan(fwd, bwd, s0_f, s0_b):
    b, n, _ = fwd[0].shape
    ns = n // STEP

    def specs(pos):
        tok = lambda wd: pl.BlockSpec((b, STEP, wd), lambda i: (0, pos(i), 0))
        return [tok(DN_W), tok(DN_W), tok(DN_W), pl.BlockSpec((b, DN_W, STEP), lambda i: (0, 0, pos(i))),
                tok(DN_W), tok(128), state]

    state = pl.BlockSpec((b, DN_HEADS, HEAD_DIM, HEAD_DIM), lambda i: (0, 0, 0, 0))
    fpos = lambda i: i
    bpos = lambda i: ns - 1 - i
    state_shape = jax.ShapeDtypeStruct((b, DN_HEADS, HEAD_DIM, HEAD_DIM), F32)
    return pl.pallas_call(
        functools.partial(_dn_scan_body, nb=b, ns=ns),
        grid=(ns,),
        in_specs=specs(fpos) + specs(bpos),
        out_specs=[pl.BlockSpec((b, STEP, DN_W), lambda i: (0, fpos(i), 0)),
                   pl.BlockSpec((b, STEP, DN_W), lambda i: (0, bpos(i), 0)), state, state],
        out_shape=[jax.ShapeDtypeStruct((b, n, DN_W), F32), jax.ShapeDtypeStruct((b, n, DN_W), F32),
                   state_shape, state_shape],
        scratch_shapes=[pltpu.VMEM((2, b, DN_HEADS, HEAD_DIM, HEAD_DIM), F32)],
        compiler_params=_cparams(("arbitrary",), 48 * 1024 * 1024),
        name="deltanet_scan",
    )(*fwd, s0_f, *bwd, s0_b)


def _deltanet(rest, restc, alog_row, dtb_row, conv_w):
    b = rest.shape[0]
    zero_state = jnp.zeros((b, DN_HEADS, HEAD_DIM, HEAD_DIM), F32)
    cf, cb = _dn_chunk(*_dn_pre(restc, alog_row, dtb_row, conv_w))
    oc_f, oc_b, sc_f, sc_b = _dn_scan(cf, cb, zero_state, zero_state)
    lf, lb = _dn_chunk(*_dn_pre(rest, alog_row, dtb_row, conv_w))
    ol_f, ol_b, _, _ = _dn_scan(lf, lb, sc_f, sc_b)
    return (ol_f, ol_b), (oc_f, oc_b)


OUTPROJ_SUB = 256


def _outproj_body(ya_ref, yg_ref, of_ref, ob_ref, z_ref, x_ref, mod_ref, dnn_ref, w_ref, gpost_ref, gpre2_ref,
                  wr_ref, x1_ref, h2_ref, aff_ref, *, tm):
    def project(rows):
        o = of_ref[0, rows, :] + ob_ref[0, rows, :]
        z = z_ref[0, rows, :]
        acc = _dot(ya_ref[0, rows, :], w_ref[0:ATTN_W, :]) + _dot(yg_ref[0, rows, :], w_ref[ATTN_W:ATTN_W + GM_W, :])
        for h in range(DN_HEADS):
            hs = slice(h * HEAD_DIM, (h + 1) * HEAD_DIM)
            oh = o[:, hs]
            yh = oh * lax.rsqrt(jnp.mean(oh * oh, axis=-1, keepdims=True) + NORM_EPS) * dnn_ref[...]
            yd = (yh * _silu(z[:, hs])).astype(BF16)
            r0 = ATTN_W + GM_W + h * HEAD_DIM
            acc = acc + _dot(yd, w_ref[r0:r0 + HEAD_DIM, :])
        return acc

    def finish(rows, acc):
        r = acc * lax.rsqrt(jnp.mean(acc * acc, axis=-1, keepdims=True) + NORM_EPS) * gpost_ref[...]
        x1 = x_ref[0, rows, :] + mod_ref[0, 2:3, :] * r
        x1_ref[0, rows, :] = x1
        y2 = x1 * lax.rsqrt(jnp.mean(x1 * x1, axis=-1, keepdims=True) + NORM_EPS) * gpre2_ref[...]
        h2 = y2 * (1.0 + mod_ref[0, 4:5, :]) + mod_ref[0, 3:4, :]
        h2_ref[0, rows, :] = h2
        logits_t = _dot_x3(wr_ref[...], h2, nt=True)
        e = jnp.exp(logits_t - jnp.max(logits_t, axis=0, keepdims=True))
        aff_ref[0, :, rows] = e / jnp.sum(e, axis=0, keepdims=True)

    sub = min(OUTPROJ_SUB, tm)
    blocks = [slice(r0, r0 + sub) for r0 in range(0, tm, sub)]
    pending = None
    for rows in blocks:
        acc = project(rows)
        if pending is not None:
            finish(*pending)
        pending = (rows, acc)
    finish(*pending)


def _outproj(ya, yg, o_f, o_b, rest, x, mod, dn_norm, w_out_bf, g_post1, g_pre2, w_router_t):
    b, n, d = x.shape
    tm = min(512, n)
    mod_map = (lambda bi, i: (bi, 0, 0)) if mod.shape[0] == b else (lambda bi, i: (0, 0, 0))
    tok = lambda w: pl.BlockSpec((1, tm, w), lambda bi, i: (bi, i, 0))
    vec = lambda w: pl.BlockSpec((1, w), lambda bi, i: (0, 0))
    return pl.pallas_call(
        functools.partial(_outproj_body, tm=tm),
        grid=(b, n // tm),
        in_specs=[tok(ATTN_W), tok(GM_W), tok(DN_W), tok(DN_W),
                  pl.BlockSpec((1, tm, DN_W), lambda bi, i: (bi, i, REST_DZ // DN_W)),
                  tok(d), pl.BlockSpec((1, 6, d), mod_map), vec(HEAD_DIM),
                  pl.BlockSpec((MIX_W, d), lambda bi, i: (0, 0), pipeline_mode=pl.Buffered(1)),
                  vec(d), vec(d), pl.BlockSpec((N_EXPERTS, d), lambda bi, i: (0, 0))],
        out_specs=[tok(d), tok(d), pl.BlockSpec((1, N_EXPERTS, tm), lambda bi, i: (bi, 0, i))],
        out_shape=[jax.ShapeDtypeStruct((b, n, d), F32), jax.ShapeDtypeStruct((b, n, d), F32),
                   jax.ShapeDtypeStruct((b, N_EXPERTS, n), F32)],
        compiler_params=_cparams(("parallel", "arbitrary"), V7X_VMEM_LIMIT),
        name="outproj_router",
    )(ya, yg, o_f, o_b, rest, x, mod, dn_norm.reshape(1, HEAD_DIM), w_out_bf, g_post1.reshape(1, d),
      g_pre2.reshape(1, d), w_router_t)


SUBLANES = 8
TOK_BITS = 12
ISSUE_GROUPS = 4


def _ffn_body(code_ref, h2_ref, gate_ref, w1_ref, w3_ref, w2_ref, ytm_ref, xs_scr, ys_scr, gsem, ssem,
              *, cap, n_exp, nb):
    e = pl.program_id(0)
    b = pl.program_id(1)
    t = e * nb + b
    slot = t & 1
    groups = cap // SUBLANES
    last = n_exp * nb - 1

    def gather(e2, b2, slot2):
        base = (b2 * n_exp + e2) * cap

        def body(gg, carry):
            for k in range(ISSUE_GROUPS):
                g = gg * ISSUE_GROUPS + k
                for j in range(SUBLANES):
                    tok = code_ref[base + g * SUBLANES + j] & ((1 << TOK_BITS) - 1)
                    pltpu.make_async_copy(h2_ref.at[b2, tok >> 3, pl.ds(tok & 7, 1)],
                                          xs_scr.at[slot2, g, pl.ds(j, 1)], gsem.at[slot2]).start()
            return carry
        lax.fori_loop(0, groups // ISSUE_GROUPS, body, 0)

    def scatter_wait(slot2):
        pltpu.make_async_copy(ys_scr.at[slot2], ytm_ref.at[pl.ds(0, groups)], ssem.at[slot2]).wait()

    @pl.when(t == 0)
    def _():
        gather(0, 0, 0)

    wrap = b + 1 == nb
    e_n = jnp.where(t < last, jnp.where(wrap, e + 1, e), e)
    b_n = jnp.where(t < last, jnp.where(wrap, 0, b + 1), b)
    base_n = (b_n * n_exp + e_n) * cap
    base = (b * n_exp + e) * cap
    row0 = b * n_exp * cap

    def gather_rows(lo_, hi_):
        for s in range(lo_, hi_):
            tok = code_ref[base_n + s] & ((1 << TOK_BITS) - 1)
            pltpu.make_async_copy(h2_ref.at[b_n, tok >> 3, pl.ds(tok & 7, 1)],
                                  xs_scr.at[1 - slot, s // SUBLANES, pl.ds(s % SUBLANES, 1)],
                                  gsem.at[1 - slot]).start()

    def scatter_rows(lo_, hi_):
        for s in range(lo_, hi_):
            p = row0 + (code_ref[base + s] >> TOK_BITS)
            pltpu.make_async_copy(ys_scr.at[slot, s // SUBLANES, pl.ds(s % SUBLANES, 1)],
                                  ytm_ref.at[p >> 3, pl.ds(p & 7, 1)], ssem.at[slot]).start(priority=s % 2)

    pltpu.make_async_copy(h2_ref.at[0, pl.ds(0, groups)], xs_scr.at[slot], gsem.at[slot]).wait()

    @pl.when(t >= 2)
    def _():
        scatter_wait(slot)

    gpr = min(groups, 256 // SUBLANES)
    rows = gpr * SUBLANES
    done = None
    for g0 in range(0, groups, gpr):
        r0 = g0 * SUBLANES
        xb = xs_scr[slot, g0:g0 + gpr].reshape(rows, -1).astype(BF16)
        a = _dot(xb, w1_ref[0])
        b3 = _dot(xb, w3_ref[0])
        gather_rows(r0, r0 + rows)
        act = (_silu(a) * b3).astype(BF16)
        y = _dot(act, w2_ref[0]) * gate_ref[0, 0, r0:r0 + rows, :]
        if done is not None:
            scatter_rows(*done)
        ys_scr[slot, g0:g0 + gpr] = y.reshape(gpr, SUBLANES, -1)
        done = (r0, r0 + rows)
    scatter_rows(*done)

    @pl.when(t == last)
    def _():
        pltpu.make_async_copy(h2_ref.at[0, pl.ds(0, groups)], xs_scr.at[1 - slot], gsem.at[1 - slot]).wait()
        if last >= 1:
            scatter_wait(1 - slot)
        scatter_wait(slot)


def _expert_ffn(code_flat, h2, gate, w1, w3, w2, layer):
    nb, n, d = h2.shape
    _, n_exp, _, ff = w1.shape
    cap = code_flat.shape[0] // (n_exp * nb)
    groups = cap // SUBLANES
    grid_spec = pltpu.PrefetchScalarGridSpec(
        num_scalar_prefetch=1,
        grid=(n_exp, nb),
        in_specs=[pl.BlockSpec(memory_space=pl.ANY),
                  pl.BlockSpec((1, 1, cap, 1), lambda e, b, code: (b, e, 0, 0)),
                  pl.BlockSpec((None, 1, d, ff), lambda e, b, code: (layer, e, 0, 0)),
                  pl.BlockSpec((None, 1, d, ff), lambda e, b, code: (layer, e, 0, 0)),
                  pl.BlockSpec((None, 1, ff, d), lambda e, b, code: (layer, e, 0, 0))],
        out_specs=pl.BlockSpec(memory_space=pl.ANY),
        scratch_shapes=[pltpu.VMEM((2, groups, SUBLANES, d), F32), pltpu.VMEM((2, groups, SUBLANES, d), F32),
                        pltpu.SemaphoreType.DMA((2,)), pltpu.SemaphoreType.DMA((2,))],
    )
    return pl.pallas_call(
        functools.partial(_ffn_body, cap=cap, n_exp=n_exp, nb=nb),
        grid_spec=grid_spec,
        out_shape=jax.ShapeDtypeStruct((nb * n_exp * groups, SUBLANES, d), F32),
        compiler_params=_cparams(("arbitrary", "arbitrary"), V7X_VMEM_LIMIT),
        name="expert_ffn",
    )(code_flat, h2.reshape(nb, n // SUBLANES, SUBLANES, d), gate, w1, w3, w2)


COMBINE_SLOTS = 4


def _combine_body(off_ref, ytm_ref, tok_ref, x1_ref, mod_ref, gpost_ref, o_ref, ybuf, sem, prog,
                  *, rows_pb, nt, nb):
    b = pl.program_id(0)
    tile = pl.program_id(1)
    d = o_ref.shape[-1]
    total = rows_pb // STEP

    def chunk_copy(c):
        slot = c & (COMBINE_SLOTS - 1)
        return pltpu.make_async_copy(ytm_ref.at[pl.ds(b * rows_pb + c * STEP, STEP)], ybuf.at[slot], sem.at[slot])

    @pl.when(tile == 0)
    def _():
        prog[0] = 0
        prog[1] = 0

    o0 = off_ref[b * (nt + 1) + tile]
    o1 = off_ref[b * (nt + 1) + tile + 1]
    c0 = o0 >> 7
    nchunks = jnp.where(o1 > o0, ((o1 + STEP - 1) >> 7) - c0, 0)

    o_ref[0] = jnp.zeros((STEP, d), F32)
    first = tok_ref[0, :, 0:1]
    end = first + tok_ref[0, :, 1:2]
    lane = lax.broadcasted_iota(I32, (STEP, STEP), 1)

    def body(i, carry):
        c = c0 + i
        issued = prog[0]
        target = jnp.minimum(c + COMBINE_SLOTS, total)

        def issue(k, carry2):
            chunk_copy(k).start()
            return carry2
        lax.fori_loop(issued, target, issue, 0)
        prog[0] = jnp.maximum(issued, target)

        @pl.when(prog[1] <= c)
        def _():
            chunk_copy(c).wait()
            prog[1] = c + 1

        posn = (c * STEP + lane).astype(F32)
        onehot = jnp.where((posn >= first) & (posn < end), 1.0, 0.0).astype(BF16)
        y = ybuf[c & (COMBINE_SLOTS - 1)]
        hi = y.astype(BF16)
        lo = (y - hi.astype(F32)).astype(BF16)
        o_ref[0] += _dot(onehot, hi) + _dot(onehot, lo)
        return carry
    lax.fori_loop(0, nchunks, body, 0)

    acc = o_ref[0]
    rr = acc * lax.rsqrt(jnp.mean(acc * acc, axis=-1, keepdims=True) + NORM_EPS) * gpost_ref[...]
    o_ref[0] = x1_ref[0] + mod_ref[0, 5:6, :] * rr


def _combine(off_flat, ytm, tokcol, x1, mod, g_post2):
    nb, n, d = x1.shape
    nt = n // STEP
    rows_pb = ytm.shape[0] // nb
    mod_map = (lambda b, i, *_: (b, 0, 0)) if mod.shape[0] == nb else (lambda b, i, *_: (0, 0, 0))
    grid_spec = pltpu.PrefetchScalarGridSpec(
        num_scalar_prefetch=1,
        grid=(nb, nt),
        in_specs=[pl.BlockSpec(memory_space=pl.ANY),
                  pl.BlockSpec((1, STEP, 128), lambda b, i, *_: (b, i, 0)),
                  pl.BlockSpec((1, STEP, d), lambda b, i, *_: (b, i, 0)),
                  pl.BlockSpec((1, 6, d), mod_map),
                  pl.BlockSpec((1, d), lambda b, i, *_: (0, 0))],
        out_specs=pl.BlockSpec((1, STEP, d), lambda b, i, *_: (b, i, 0)),
        scratch_shapes=[pltpu.VMEM((COMBINE_SLOTS, STEP, d), F32), pltpu.SemaphoreType.DMA((COMBINE_SLOTS,)),
                        pltpu.SMEM((2,), I32)],
    )
    return pl.pallas_call(
        functools.partial(_combine_body, rows_pb=rows_pb, nt=nt, nb=nb),
        grid_spec=grid_spec,
        out_shape=jax.ShapeDtypeStruct((nb, n, d), F32),
        compiler_params=_cparams(("arbitrary", "arbitrary"), 40 * 1024 * 1024),
        name="moe_combine",
    )(off_flat, ytm, tokcol, x1, mod, g_post2.reshape(1, d))


ROUTE_BISECTIONS = 24


def _route_body(aff_ref, code_ref, gate_ref, tokcol_ref, off_ref, slot_scr, sel_scr, ph_scr, pl_scr, gh_scr,
                gm_scr, gl_scr, *, n, cap):
    n_exp = N_EXPERTS
    nblk = n // STEP
    aff = aff_ref[0]
    ones_where = lambda m: jnp.where(m, 1.0, 0.0).astype(F32)

    thr = jnp.zeros((n_exp, 1), I32)
    for bit in range(30, -1, -1):
        cand = thr | (1 << bit)
        enough = jnp.sum(ones_where(aff >= pltpu.bitcast(cand, F32)), axis=1, keepdims=True) >= cap
        thr = jnp.where(enough, cand, thr)
    lo = pltpu.bitcast(thr, F32)
    hi = pltpu.bitcast(thr + 1, F32)
    for _ in range(ROUTE_BISECTIONS):
        mid = 0.5 * (lo + hi)
        enough = jnp.sum(ones_where(aff >= mid), axis=1, keepdims=True) >= cap
        lo = jnp.where(enough, mid, lo)
        hi = jnp.where(enough, hi, mid)
    gt = aff >= hi
    eq = (aff >= lo) & (aff < hi)
    need = cap - jnp.sum(ones_where(gt), axis=1, keepdims=True)
    tl = lax.broadcasted_iota(I32, (n_exp, n), 1)
    last = jnp.zeros((n_exp, 1), I32)
    for bit in range(n.bit_length() - 1, -1, -1):
        cand = last | (1 << bit)
        c = jnp.sum(ones_where(eq & (tl < cand)), axis=1, keepdims=True)
        last = jnp.where(c < need, cand, last)
    sel = ones_where(gt | (eq & (tl <= last) & (need > 0)))

    r16 = lax.broadcasted_iota(I32, (n_exp, n_exp), 0)
    c16 = lax.broadcasted_iota(I32, (n_exp, n_exp), 1)
    rank = _dot(jnp.where(c16 < r16, 1.0, 0.0).astype(BF16), sel.astype(BF16))
    cnt = jnp.sum(sel, axis=0, keepdims=True)

    rt = lax.broadcasted_iota(I32, (STEP, STEP), 0)
    ct = lax.broadcasted_iota(I32, (STEP, STEP), 1)
    upper = jnp.where(rt <= ct, 1.0, 0.0).astype(BF16)
    stacked = jnp.concatenate([sel, jnp.broadcast_to(cnt, (SUBLANES, n))], axis=0)
    base = jnp.zeros((n_exp + SUBLANES, 1), F32)
    lane = lax.broadcasted_iota(I32, (1, 128), 1)
    row8 = lax.broadcasted_iota(I32, (SUBLANES, STEP), 0)
    offs = jnp.zeros((1, 128), F32)
    for j in range(nblk):
        blk = slice(j * STEP, (j + 1) * STEP)
        incl = _dot(stacked[:, blk].astype(BF16), upper) + base
        offs = jnp.where(lane == j, base[n_exp:n_exp + 1, :], offs)
        base = incl[:, STEP - 1:STEP]
        cnt_b = cnt[:, blk]
        first_b = incl[n_exp:n_exp + 1, :] - cnt_b
        pos_b = first_b + rank[:, blk]
        pos_hi = jnp.floor(pos_b * (1.0 / STEP))
        slot_scr[:, blk] = incl[:n_exp, :] - 1.0
        sel_scr[:, blk] = sel[:, blk]
        ph_scr[:, blk] = pos_hi
        pl_scr[:, blk] = pos_b - pos_hi * STEP
        g = aff[:, blk]
        gh = g.astype(BF16).astype(F32)
        gm = (g - gh).astype(BF16).astype(F32)
        gh_scr[:, blk] = gh
        gm_scr[:, blk] = gm
        gl_scr[:, blk] = g - gh - gm
        cols = jnp.where(row8 == 0, first_b, jnp.where(row8 == 1, cnt_b, 0.0))
        tokcol_ref[0, blk, :] = jnp.concatenate([cols, jnp.zeros((STEP - SUBLANES, STEP), F32)], axis=0).T
    offs = jnp.where(lane == nblk, base[n_exp:n_exp + 1, :], offs)
    off_ref[0] = offs.astype(I32)

    s_iota = lax.broadcasted_iota(I32, (cap, STEP), 0).astype(F32)
    tok_lo = lax.broadcasted_iota(I32, (1, STEP), 1).astype(F32)

    code_ref[0] = jnp.zeros((n_exp, cap), I32)
    gate_ref[0] = jnp.zeros((n_exp, cap), F32)
    rowc = lax.broadcasted_iota(I32, (SUBLANES, cap), 0)

    def per_expert(e, carry):
        e8 = pl.multiple_of((e >> 3) << 3, SUBLANES)
        mine8 = row8 == (e & 7)
        acc = jnp.zeros((SUBLANES, cap), F32)
        for j in range(nblk):
            blk = slice(j * STEP, (j + 1) * STEP)
            row = lambda scr: jnp.sum(jnp.where(mine8, scr[pl.ds(e8, SUBLANES), blk], 0.0), axis=0, keepdims=True)
            onehot = jnp.where((row(slot_scr) == s_iota) & (row(sel_scr) > 0.0), 1.0, 0.0).astype(BF16)
            vals = jnp.where(row8 == 0, float(j), jnp.where(row8 == 1, tok_lo, jnp.where(
                row8 == 2, row(ph_scr), jnp.where(row8 == 3, row(pl_scr), jnp.where(
                    row8 == 4, row(gh_scr), jnp.where(row8 == 5, row(gm_scr), jnp.where(
                        row8 == 6, row(gl_scr), 0.0)))))))
            acc = acc + _dot_nt(vals.astype(BF16), onehot)
        ints = acc[0:4, :].astype(I32)
        token = ints[0:1] * STEP + ints[1:2]
        position = ints[2:3] * STEP + ints[3:4]
        minec = rowc == (e & 7)
        code_ref[0, pl.ds(e8, SUBLANES), :] = jnp.where(minec, (position << TOK_BITS) | token,
                                                        code_ref[0, pl.ds(e8, SUBLANES), :])
        gate_ref[0, pl.ds(e8, SUBLANES), :] = jnp.where(minec, (acc[4:5] + acc[5:6]) + acc[6:7],
                                                        gate_ref[0, pl.ds(e8, SUBLANES), :])
        return carry
    lax.fori_loop(0, n_exp, per_expert, 0)


def _route(aff_t, cap):
    b, n_exp, n = aff_t.shape
    assert n <= 1 << TOK_BITS and n // STEP < 128
    scr = pltpu.VMEM((n_exp, n), F32)
    return pl.pallas_call(
        functools.partial(_route_body, n=n, cap=cap),
        grid=(b,),
        in_specs=[pl.BlockSpec((1, n_exp, n), lambda bi: (bi, 0, 0))],
        out_specs=[pl.BlockSpec((1, n_exp, cap), lambda bi: (bi, 0, 0)),
                   pl.BlockSpec((1, n_exp, cap), lambda bi: (bi, 0, 0)),
                   pl.BlockSpec((1, n, 128), lambda bi: (bi, 0, 0)),
                   pl.BlockSpec((1, 1, 128), lambda bi: (bi, 0, 0))],
        out_shape=[jax.ShapeDtypeStruct((b, n_exp, cap), I32), jax.ShapeDtypeStruct((b, n_exp, cap), F32),
                   jax.ShapeDtypeStruct((b, n, 128), F32), jax.ShapeDtypeStruct((b, 1, 128), I32)],
        scratch_shapes=[scr] * 7,
        compiler_params=_cparams(("arbitrary",), 40 * 1024 * 1024),
        name="moe_route",
    )(aff_t)


def _moe(aff_t, h2, x1, mod, g_post2, w1, w3, w2, layer):
    b, n, d = x1.shape
    cap = EC_CAPACITY * n // N_EXPERTS
    code, gate, tokcol, off = _route(aff_t, cap)
    ytm = _expert_ffn(code.reshape(-1), h2, gate[..., None], w1, w3, w2, layer)
    rows = b * N_EXPERTS * cap
    return _combine(off[:, 0, :n // STEP + 1].reshape(-1), ytm.reshape(rows, d), tokcol, x1, mod, g_post2)


def _rope_tables(n):
    rows = n // GRID_W
    r = jnp.repeat(jnp.arange(rows, dtype=F32), GRID_W)
    col = jnp.tile(jnp.arange(GRID_W, dtype=F32), rows)
    half = HEAD_DIM // 2
    inv = ROPE_BASE ** (-jnp.arange(0, half, 2, dtype=F32) / half)
    ar, ac = r[:, None] * inv, col[:, None] * inv
    cr, sr, cc, sc = jnp.cos(ar), jnp.sin(ar), jnp.cos(ac), jnp.sin(ac)
    return (jnp.concatenate([cr, cr, cc, cc], axis=-1), jnp.concatenate([-sr, sr, -sc, sc], axis=-1))


def _prep_w_in(w_in):
    d = w_in.shape[0]
    main = w_in[:, :QKV_W + REST_GATE]
    gcols = w_in[:, QKV_W + REST_GATE:].reshape(d, 2, 2, DN_HEADS).transpose(0, 2, 1, 3).reshape(d, 4 * DN_HEADS)
    pad = jnp.zeros((d, IN_PAD_W - QKV_W - REST_GATE - 4 * DN_HEADS), w_in.dtype)
    return jnp.concatenate([main, gcols, pad], axis=1).astype(BF16)


def _lane_row(v8):
    return jnp.concatenate([v8.reshape(-1).astype(F32), jnp.zeros((128 - v8.size,), F32)]).reshape(1, 128)


def kernel(x, c, ctx, c_ctx, g_pre1, g_post1, g_pre2, g_post2, w_mod, b_mod, w_in, attn_sink, gm_ln, gm_ws, gm_bs,
           dn_conv, dn_a_log, dn_dt_bias, dn_norm, w_out, w_router, w_e1, w_e3, w_e2):
    b, n, d = x.shape
    depth = w_in.shape[0]
    rope_cs = _rope_tables(n)
    c8 = jnp.concatenate([c, c_ctx[None, :], jnp.zeros((8 - b - 1, d), F32)], axis=0)
    w1, w3, w2 = w_e1.astype(BF16), w_e3.astype(BF16), w_e2.astype(BF16)
    xc = ctx
    for l in range(depth):
        update_ctx = l < depth - 1
        mod_all = _modulation(c8, w_mod, b_mod[l], l)
        mod_lat = mod_all[:b].reshape(b, 6, d)
        mod_ctx = mod_all[b:b + 1].reshape(1, 6, d)
        w_in_bf = _prep_w_in(w_in[l])
        qkv, rest = _inproj(x, mod_lat, g_pre1[l], w_in_bf, rope_cs)
        qkvc, restc = _inproj(xc, mod_ctx, g_pre1[l], w_in_bf, None)
        ya = _attention(qkv, qkvc, attn_sink[l])
        ws_bf = gm_ws[l].astype(BF16)
        yg = _gmlp(rest, gm_ln[l], ws_bf, gm_bs[l])
        (o_f, o_b), (oc_f, oc_b) = _deltanet(rest, restc, _lane_row(dn_a_log[l]), _lane_row(dn_dt_bias[l]),
                                             dn_conv[l])
        w_out_bf = w_out[l].astype(BF16)
        w_router_t = w_router[l].T
        x1, h2, aff_t = _outproj(ya, yg, o_f, o_b, rest, x, mod_lat, dn_norm[l], w_out_bf, g_post1[l], g_pre2[l],
                                 w_router_t)
        x = _moe(aff_t, h2, x1, mod_lat, g_post2[l], w1, w3, w2, l)
        if update_ctx:
            yac = _ctx_attention(qkvc, attn_sink[l])
            ygc = _gmlp(restc, gm_ln[l], ws_bf, gm_bs[l])
            xc1, h2c, affc_t = _outproj(yac, ygc, oc_f, oc_b, restc, xc, mod_ctx, dn_norm[l], w_out_bf, g_post1[l],
                                        g_pre2[l], w_router_t)
            xc = _moe(affc_t, h2c, xc1, mod_ctx, g_post2[l], w1, w3, w2, l)
    return x
```
